```python
import jax, jax.numpy as jnp
from jax import lax
import numpy as np

D_MODEL = 2048
BATCH = 4
SEQ = 2048
DEPTH = 4
DEC_BATCH = 8
DEC_SEQ = 1
PAST_LEN = 16384
PAGE_SIZE = 128

N_A_LAYERS = DEPTH // 2
N_B_LAYERS = DEPTH - N_A_LAYERS
GLA_HEADS = 4
GLA_DK = (D_MODEL // 2) // GLA_HEADS
GLA_DV = D_MODEL // GLA_HEADS
GLA_QK_WIDTH = GLA_HEADS * GLA_DK
GLA_V_WIDTH = GLA_HEADS * GLA_DV
GLA_GATE_RANK = 16
GLA_GATE_TAU = 16.0
GLA_CHUNK = 64
GLA_IN_DIM = 2 * GLA_QK_WIDTH + 2 * GLA_V_WIDTH + GLA_GATE_RANK
MOBA_HEADS = 16
MOBA_HEAD_DIM = D_MODEL // MOBA_HEADS
MOBA_WIDTH = MOBA_HEADS * MOBA_HEAD_DIM
MOBA_BLOCK = 256
MOBA_TOP_K = 3
MOBA_Q_BLOCK = 16
D_FF = 4 * D_MODEL
RMS_EPS = 1e-6
F32 = jnp.float32

kernel_name = 'hybrid_gla_moba_yoco_step'


def rms_norm(x, w):
    xf = x.astype(F32)
    y = xf * lax.rsqrt(jnp.mean(xf * xf, axis=-1, keepdims=True) + RMS_EPS)
    return (y * w.astype(F32)).astype(x.dtype)


def sq_relu_mlp(x, w_up, w_down):
    h = jax.nn.relu(x @ w_up)
    return (h * h) @ w_down


def _pad_time(a, pad):
    if pad == 0:
        return a
    return jnp.pad(a, [(0, 0), (0, pad)] + [(0, 0)] * (a.ndim - 2))


def gla_scan(q, k, v, g, s0):
    B, T, H, DK = q.shape
    DV = v.shape[-1]
    c = min(GLA_CHUNK, T)
    n = -(-T // c)
    pad = n * c - T

    def chunks(a):
        a = _pad_time(a.astype(F32), pad)
        return a.reshape(B, n, c, H, a.shape[-1]).swapaxes(0, 1)

    causal = jnp.tril(jnp.ones((c, c), dtype=bool))[None, :, :, None, None]

    def step(S, inp):
        qc, kc, vc, gc = inp
        b = jnp.cumsum(gc, axis=1)
        o_inter = jnp.einsum('bthk,bhkv->bthv', qc * jnp.exp(b), S)
        decay = jnp.exp(jnp.where(causal, b[:, :, None] - b[:, None, :], -jnp.inf))
        att = jnp.einsum('bthk,bshk,btshk->bths', qc, kc, decay)
        o_intra = jnp.einsum('bths,bshv->bthv', att, vc)
        b_end = b[:, -1]
        S = jnp.exp(b_end)[..., None] * S + jnp.einsum(
            'bshk,bshv->bhkv', kc * jnp.exp(b_end[:, None] - b), vc)
        return S, o_inter + o_intra

    S_fin, o = lax.scan(step, s0.astype(F32), (chunks(q), chunks(k), chunks(v), chunks(g)))
    o = o.swapaxes(0, 1).reshape(B, n * c, H, DV)[:, :T]
    return o, S_fin


def gla_mixer(x, w_in, w_gate, b_gate, head_norm, w_out, s0):
    B, T, _ = x.shape
    proj = x @ w_in
    q, k, v, r, z = jnp.split(
        proj,
        [GLA_QK_WIDTH, 2 * GLA_QK_WIDTH, 2 * GLA_QK_WIDTH + GLA_V_WIDTH,
         2 * GLA_QK_WIDTH + 2 * GLA_V_WIDTH],
        axis=-1)
    q = q.reshape(B, T, GLA_HEADS, GLA_DK) * (GLA_DK ** -0.5)
    k = k.reshape(B, T, GLA_HEADS, GLA_DK)
    v = v.reshape(B, T, GLA_HEADS, GLA_DV)
    g = jax.nn.log_sigmoid((z @ w_gate + b_gate).astype(F32)) / GLA_GATE_TAU
    g = g.reshape(B, T, GLA_HEADS, GLA_DK)
    o, s = gla_scan(q, k, v, g, s0)
    o = rms_norm(o, head_norm).reshape(B, T, GLA_V_WIDTH).astype(x.dtype)
    return (o * jax.nn.silu(r)) @ w_out, s.astype(s0.dtype)


def moba_attention(q, k_all, v_all):
    B, T, H, hd = q.shape
    L = k_all.shape[1]
    q_pos0 = L - T
    nb = -(-L // MOBA_BLOCK)
    kpad = nb * MOBA_BLOCK - L
    kb = _pad_time(k_all, kpad).reshape(B, nb, MOBA_BLOCK, H, hd)
    vb = _pad_time(v_all, kpad).reshape(B, nb, MOBA_BLOCK, H, hd)
    kmean = jnp.mean(kb.astype(F32), axis=2)
    n_sel = min(MOBA_TOP_K, nb)
    qb = min(MOBA_Q_BLOCK, T)
    nq = -(-T // qb)
    q_chunks = _pad_time(q, nq * qb - T).reshape(B, nq, qb, H, hd).swapaxes(0, 1)
    pos = (q_pos0 + jnp.arange(nq * qb, dtype=jnp.int32)).reshape(nq, qb)
    b_ix = jnp.arange(B)[:, None, None, None]
    h_ix = jnp.arange(H)[None, None, :, None]
    offs = jnp.arange(MOBA_BLOCK, dtype=jnp.int32)
    scale = hd ** -0.5

    def attend(args):
        qc, pc = args
        j = pc // MOBA_BLOCK
        gate = jnp.einsum('bqhd,bnhd->bqhn', qc.astype(F32), kmean)
        past = jnp.arange(nb)[None, :] < j[:, None]
        gate = jnp.where(past[None, :, None, :], gate, -jnp.inf)
        _, top_idx = lax.top_k(gate, n_sel)
        valid = top_idx < j[None, :, None, None]
        own = jnp.broadcast_to(jnp.minimum(j, nb - 1)[None, :, None, None], (B, qb, H, 1))
        idx = jnp.concatenate([top_idx, own], axis=-1)
        keep = jnp.concatenate([valid, jnp.ones((B, qb, H, 1), dtype=bool)], axis=-1)
        kg = kb[b_ix, idx, :, h_ix]
        vg = vb[b_ix, idx, :, h_ix]
        s = jnp.einsum('bqhd,bqhskd->bqhsk', qc, kg).astype(F32) * scale
        key_pos = idx[..., None] * MOBA_BLOCK + offs
        mask = keep[..., None] & (key_pos <= pc[None, :, None, None, None])
        p = jax.nn.softmax(jnp.where(mask, s, -jnp.inf), axis=(-2, -1))
        return jnp.einsum('bqhsk,bqhskd->bqhd', p.astype(vg.dtype), vg)

    o = lax.map(attend, (q_chunks, pos))
    return o.swapaxes(0, 1).reshape(B, nq * qb, H, hd)[:, :T]


def moba_mixer(x, k_all, v_all, w_q, w_o):
    B, T, _ = x.shape
    q = (x @ w_q).reshape(B, T, MOBA_HEADS, MOBA_HEAD_DIM)
    o = moba_attention(q, k_all, v_all)
    return o.reshape(B, T, MOBA_WIDTH) @ w_o


def _trunk(x, gla_s0, k_past, v_past, norm_mix, norm_mlp, w_mlp_up, w_mlp_down,
           w_in_a, w_gate_a, b_gate_a, head_norm_a, w_out_a, kv_norm, w_kv,
           w_q_b, w_o_b, norm_final):
    B, T, _ = x.shape
    h = x
    gla_states = []
    k_new = v_new = k_all = v_all = None
    for l in range(DEPTH):
        hn = rms_norm(h, norm_mix[l])
        if l < N_A_LAYERS:
            y, s = gla_mixer(hn, w_in_a[l], w_gate_a[l], b_gate_a[l], head_norm_a[l],
                             w_out_a[l], gla_s0[l])
            gla_states.append(s)
        else:
            i = l - N_A_LAYERS
            y = moba_mixer(hn, k_all, v_all, w_q_b[i], w_o_b[i])
        h = h + y
        h = h + sq_relu_mlp(rms_norm(h, norm_mlp[l]), w_mlp_up[l], w_mlp_down[l])
        if l == N_A_LAYERS - 1:
            kv = rms_norm(h, kv_norm) @ w_kv
            k_new = kv[..., :MOBA_WIDTH].reshape(B, T, MOBA_HEADS, MOBA_HEAD_DIM)
            v_new = kv[..., MOBA_WIDTH:].reshape(B, T, MOBA_HEADS, MOBA_HEAD_DIM)
            if k_past is None:
                k_all, v_all = k_new, v_new
            else:
                k_all = jnp.concatenate([k_past.astype(k_new.dtype), k_new], axis=1)
                v_all = jnp.concatenate([v_past.astype(v_new.dtype), v_new], axis=1)
    return rms_norm(h, norm_final), jnp.stack(gla_states), k_new, v_new


def setup_inputs(seed: int = 0) -> dict:
    key = jax.random.key(seed)
    ks = jax.random.split(key, 20)
    n_pages = PAST_LEN // PAGE_SIZE
    n_used = DEC_BATCH * n_pages
    n_phys = n_used + max(1, n_used // 4)

    def nrm(k, shape, scale):
        return jax.random.normal(k, shape, jnp.float32) * scale

    return {
        'x_prompt': nrm(ks[0], (BATCH, SEQ, D_MODEL), 1.0),
        'x_sample': nrm(ks[1], (DEC_BATCH, DEC_SEQ, D_MODEL), 1.0),
        'state_gla': nrm(ks[2], (N_A_LAYERS, DEC_BATCH, GLA_HEADS, GLA_DK, GLA_DV), 1.0),
        'cache_k': nrm(ks[3], (n_phys, PAGE_SIZE, MOBA_HEADS, MOBA_HEAD_DIM), 1.0),
        'cache_v': nrm(ks[4], (n_phys, PAGE_SIZE, MOBA_HEADS, MOBA_HEAD_DIM), 1.0),
        'page_table': jax.random.permutation(ks[5], n_phys)[:n_used].reshape(DEC_BATCH, n_pages).astype(jnp.int32),
        'norm_mix': 1.0 + nrm(ks[6], (DEPTH, D_MODEL), 0.02),
        'norm_mlp': 1.0 + nrm(ks[7], (DEPTH, D_MODEL), 0.02),
        'w_mlp_up': nrm(ks[8], (DEPTH, D_MODEL, D_FF), D_MODEL ** -0.5),
        'w_mlp_down': nrm(ks[9], (DEPTH, D_FF, D_MODEL), D_FF ** -0.5),
        'w_in_a': nrm(ks[10], (N_A_LAYERS, D_MODEL, GLA_IN_DIM), D_MODEL ** -0.5),
        'w_gate_a': nrm(ks[11], (N_A_LAYERS, GLA_GATE_RANK, GLA_QK_WIDTH), GLA_GATE_RANK ** -0.5),
        'b_gate_a': nrm(ks[12], (N_A_LAYERS, GLA_QK_WIDTH), 0.1),
        'head_norm_a': 1.0 + nrm(ks[13], (N_A_LAYERS, GLA_DV), 0.02),
        'w_out_a': nrm(ks[14], (N_A_LAYERS, GLA_V_WIDTH, D_MODEL), GLA_V_WIDTH ** -0.5),
        'kv_norm': 1.0 + nrm(ks[15], (D_MODEL,), 0.02),
        'w_kv': nrm(ks[16], (D_MODEL, 2 * MOBA_WIDTH), D_MODEL ** -0.5),
        'w_q_b': nrm(ks[17], (N_B_LAYERS, D_MODEL, MOBA_WIDTH), D_MODEL ** -0.5),
        'w_o_b': nrm(ks[18], (N_B_LAYERS, MOBA_WIDTH, D_MODEL), MOBA_WIDTH ** -0.5),
        'norm_final': 1.0 + nrm(ks[19], (D_MODEL,), 0.02),
    }


def reference(x_prompt, x_sample, state_gla, cache_k, cache_v, page_table,
              norm_mix, norm_mlp, w_mlp_up, w_mlp_down, w_in_a, w_gate_a, b_gate_a,
              head_norm_a, w_out_a, kv_norm, w_kv, w_q_b, w_o_b, norm_final):
    s0_prompt = jnp.zeros((N_A_LAYERS, x_prompt.shape[0], GLA_HEADS, GLA_DK, GLA_DV), state_gla.dtype)
    y_prompt, gla_state_prompt, k_prompt, v_prompt = _trunk(
        x_prompt, s0_prompt, None, None, norm_mix, norm_mlp, w_mlp_up, w_mlp_down,
        w_in_a, w_gate_a, b_gate_a, head_norm_a, w_out_a, kv_norm, w_kv, w_q_b, w_o_b,
        norm_final)
    dec_b, n_pages = page_table.shape
    past_len = n_pages * cache_k.shape[1]
    k_past = cache_k[page_table].reshape(dec_b, past_len, MOBA_HEADS, MOBA_HEAD_DIM)
    v_past = cache_v[page_table].reshape(dec_b, past_len, MOBA_HEADS, MOBA_HEAD_DIM)
    y_sample, gla_state_sample, k_sample, v_sample = _trunk(
        x_sample, state_gla, k_past, v_past, norm_mix, norm_mlp, w_mlp_up, w_mlp_down,
        w_in_a, w_gate_a, b_gate_a, head_norm_a, w_out_a, kv_norm, w_kv, w_q_b, w_o_b,
        norm_final)
    return (y_prompt, y_sample, gla_state_prompt, gla_state_sample, k_prompt, v_prompt, k_sample, v_sample)
```

```python
import functools

import jax
import jax.numpy as jnp
from jax import lax
from jax.experimental import pallas as pl
from jax.experimental.pallas import tpu as pltpu

F32 = jnp.float32
BF16 = jnp.bfloat16
HIGHEST = lax.Precision.HIGHEST

RMS_EPS = 1e-6
GLA_HEADS = 4
GLA_GATE_RANK = 16
GLA_GATE_TAU = 16.0
GLA_CHUNK = 64
GLA_SUB = 16
MOBA_HEADS = 16
MOBA_BLOCK = 256
MOBA_TOP_K = 3

V7X_VMEM_BYTES = 64 * 1024 * 1024
VMEM_LIMIT_BYTES = V7X_VMEM_BYTES - 8 * 1024 * 1024


def _params(*semantics):
    return pltpu.CompilerParams(dimension_semantics=semantics,
                                vmem_limit_bytes=VMEM_LIMIT_BYTES)


def _rms_normalize(x, gain):
    ms = jnp.mean(x * x, axis=-1, keepdims=True)
    return x * lax.rsqrt(ms + RMS_EPS) * gain


def _row_tile(m, want):
    return want if m % want == 0 else m


def _norm_matmul_kernel(x_ref, gain_ref, w_ref, o_ref, xn_ref):
    @pl.when(pl.program_id(1) == 0)
    def _():
        xn_ref[...] = _rms_normalize(x_ref[...], gain_ref[...]).astype(xn_ref.dtype)

    o_ref[...] = jnp.dot(xn_ref[...], w_ref[...],
                         preferred_element_type=F32).astype(o_ref.dtype)


def norm_matmul(x, gain, w, *, n_split=1, tm=512, tn=512):
    m, d = x.shape
    n = w.shape[1]
    n_out = n // n_split
    tm = _row_tile(m, tm)
    tn = min(tn, n_out)
    assert n_out % tn == 0 and n == n_out * n_split
    per_split = n_out // tn
    return pl.pallas_call(
        _norm_matmul_kernel,
        grid=(m // tm, n // tn),
        in_specs=[
            pl.BlockSpec((tm, d), lambda i, j: (i, 0)),
            pl.BlockSpec((1, d), lambda i, j: (0, 0)),
            pl.BlockSpec((d, tn), lambda i, j: (0, j)),
        ],
        out_specs=pl.BlockSpec((None, tm, tn),
                               lambda i, j: (j // per_split, i, j % per_split)),
        out_shape=jax.ShapeDtypeStruct((n_split, m, n_out), F32),
        scratch_shapes=[pltpu.VMEM((tm, d), BF16)],
        compiler_params=_params("parallel", "arbitrary"),
        name="norm_matmul",
    )(x, gain.reshape(1, d), w)


def _gla_gate_kernel(x_ref, gain_ref, wz_ref, wg_ref, bg_ref, o_ref):
    xn = _rms_normalize(x_ref[...], gain_ref[...])
    z = jnp.dot(xn, wz_ref[...], precision=HIGHEST, preferred_element_type=F32)
    logit = jnp.dot(z, wg_ref[...], precision=HIGHEST,
                    preferred_element_type=F32) + bg_ref[...]
    log_sig = jnp.minimum(logit, 0.0) - jnp.log(1.0 + jnp.exp(-jnp.abs(logit)))
    o_ref[...] = log_sig / GLA_GATE_TAU


def gla_gate(x, gain, w_z, w_gate, b_gate, *, tm=256):
    m, d = x.shape
    rank, n = w_gate.shape
    tm = _row_tile(m, tm)
    return pl.pallas_call(
        _gla_gate_kernel,
        grid=(m // tm,),
        in_specs=[
            pl.BlockSpec((tm, d), lambda i: (i, 0)),
            pl.BlockSpec((1, d), lambda i: (0, 0)),
            pl.BlockSpec((d, rank), lambda i: (0, 0)),
            pl.BlockSpec((rank, n), lambda i: (0, 0)),
            pl.BlockSpec((1, n), lambda i: (0, 0)),
        ],
        out_specs=pl.BlockSpec((tm, n), lambda i: (i, 0)),
        out_shape=jax.ShapeDtypeStruct((m, n), F32),
        compiler_params=_params("parallel"),
        name="gla_gate",
    )(x, gain.reshape(1, d), w_z, w_gate, b_gate.reshape(1, n))


def _contract_last(a, b):
    return lax.dot_general(a, b, (((1,), (1,)), ((), ())), preferred_element_type=F32)


def _contract_first(a, b):
    return lax.dot_general(a, b, (((0,), (0,)), ((), ())), preferred_element_type=F32)


def _gla_chunk(q, k, v, g, st):
    c, dk = q.shape
    n_sub = c // GLA_SUB
    row = lax.broadcasted_iota(jnp.int32, (c, c), 0)
    col = lax.broadcasted_iota(jnp.int32, (c, c), 1)
    tril = jnp.where(row >= col, 1.0, 0.0).astype(F32)
    b = jnp.dot(tril, g, precision=HIGHEST, preferred_element_type=F32)
    sub = lax.broadcasted_iota(jnp.int32, (c, 1), 0) // GLA_SUB

    def rows_of(r):
        return jnp.concatenate(
            [jnp.broadcast_to(b[j * GLA_SUB + r:j * GLA_SUB + r + 1, :], (GLA_SUB, dk))
             for j in range(n_sub)], axis=0)

    b_sub_end = rows_of(GLA_SUB - 1)
    b_sub_start = rows_of(0)
    b_end = b[c - 1:c, :]

    o = _contract_last((q * jnp.exp(b)).astype(BF16), st.astype(BF16))

    q_diag = q * jnp.exp(b - b_sub_start)
    k_diag = k * jnp.exp(b_sub_start - b)
    att_diag = _contract_last(q_diag.astype(BF16), k_diag.astype(BF16))
    same_sub = (row // GLA_SUB) == (col // GLA_SUB)
    att = jnp.where(same_sub & (row >= col), att_diag, 0.0)

    if n_sub > 1:
        k_off = k * jnp.exp(b_sub_end - b)
        q_parts, k_parts = [], []
        for j in range(n_sub - 1):
            b_j_end = b[(j + 1) * GLA_SUB - 1:(j + 1) * GLA_SUB, :]
            q_j = jnp.where(sub > j, q * jnp.exp(jnp.minimum(b - b_j_end, 0.0)), 0.0)
            q_parts.append(q_j.astype(BF16))
            k_parts.append(jnp.where(sub == j, k_off, 0.0).astype(BF16))
        att = att + _contract_last(jnp.concatenate(q_parts, axis=1),
                                   jnp.concatenate(k_parts, axis=1))

    v_b = v.astype(BF16)
    o = o + jnp.dot(att.astype(BF16), v_b, preferred_element_type=F32)

    k_end = (k * jnp.exp(b_end - b)).astype(BF16)
    st_new = st * jnp.exp(b_end) + _contract_first(v_b, k_end)
    return o, st_new


def _gla_prompt_kernel(q_ref, k_ref, v_ref, r_ref, g_ref, hn_ref, og_ref, s_ref, st_ref,
                       *, n_chunks, q_scale):
    t = pl.program_id(2)

    @pl.when(t == 0)
    def _():
        st_ref[...] = jnp.zeros_like(st_ref)

    def chunk(ci, carry):
        rows = pl.ds(pl.multiple_of(ci * GLA_CHUNK, GLA_CHUNK), GLA_CHUNK)
        o, st_new = _gla_chunk(q_ref[rows, :] * q_scale, k_ref[rows, :], v_ref[rows, :],
                               g_ref[rows, :], st_ref[...])
        st_ref[...] = st_new
        r = r_ref[rows, :]
        silu = r / (1.0 + jnp.exp(-r))
        og_ref[rows, :] = (_rms_normalize(o, hn_ref[...]) * silu).astype(og_ref.dtype)
        return carry

    lax.fori_loop(0, n_chunks, chunk, 0)

    @pl.when(t == pl.num_programs(2) - 1)
    def _():
        s_ref[...] = st_ref[...].T


def gla_prompt(proj, g, head_norm, *, batch, seq, tt=512):
    m = batch * seq
    qk = g.shape[1]
    dk = qk // GLA_HEADS
    vw = (proj.shape[1] - 2 * qk) // 2
    dv = vw // GLA_HEADS
    tt = min(tt, seq)
    assert seq % tt == 0 and tt % GLA_CHUNK == 0 and dv % dk == 0
    nt = seq // tt
    k_col0 = qk // dk
    v_col0 = 2 * qk // dv
    r_col0 = (2 * qk + vw) // dv
    kern = functools.partial(_gla_prompt_kernel, n_chunks=tt // GLA_CHUNK, q_scale=dk ** -0.5)
    return pl.pallas_call(
        kern,
        grid=(batch, GLA_HEADS, nt),
        in_specs=[
            pl.BlockSpec((tt, dk), lambda b, h, t: (b * nt + t, h)),
            pl.BlockSpec((tt, dk), lambda b, h, t: (b * nt + t, k_col0 + h)),
            pl.BlockSpec((tt, dv), lambda b, h, t: (b * nt + t, v_col0 + h)),
            pl.BlockSpec((tt, dv), lambda b, h, t: (b * nt + t, r_col0 + h)),
            pl.BlockSpec((tt, dk), lambda b, h, t: (b * nt + t, h)),
            pl.BlockSpec((1, dv), lambda b, h, t: (0, 0)),
        ],
        out_specs=[
            pl.BlockSpec((tt, dv), lambda b, h, t: (b * nt + t, h)),
            pl.BlockSpec((None, None, dk, dv), lambda b, h, t: (b, h, 0, 0)),
        ],
        out_shape=[
            jax.ShapeDtypeStruct((m, vw), BF16),
            jax.ShapeDtypeStruct((batch, GLA_HEADS, dk, dv), F32),
        ],
        scratch_shapes=[pltpu.VMEM((dv, dk), F32)],
        compiler_params=_params("parallel", "parallel", "arbitrary"),
        name="gla_prompt",
    )(proj, proj, proj, proj, g, head_norm.reshape(1, dv))


def _gla_step_kernel(q_ref, k_ref, g_ref, v_ref, r_ref, hn_ref, s0_ref, og_ref, s_ref,
                     *, q_scale):
    s_new = jnp.exp(g_ref[...]) * s0_ref[...] + k_ref[...] * v_ref[...]
    s_ref[...] = s_new
    o = jnp.sum((q_ref[...] * q_scale) * s_new, axis=0, keepdims=True)
    r = r_ref[...]
    silu = r / (1.0 + jnp.exp(-r))
    og_ref[...] = _rms_normalize(o, hn_ref[...]) * silu


def gla_step(proj, g, head_norm, s0):
    batch = proj.shape[0]
    qk = g.shape[1]
    dk = qk // GLA_HEADS
    vw = (proj.shape[1] - 2 * qk) // 2
    dv = vw // GLA_HEADS
    col = lambda a: a.reshape(batch, GLA_HEADS, dk, 1)
    row = lambda a: a.reshape(batch, GLA_HEADS, 1, dv)
    q_c = col(proj[:, :qk])
    k_c = col(proj[:, qk:2 * qk])
    g_c = col(g)
    v_r = row(proj[:, 2 * qk:2 * qk + vw])
    r_r = row(proj[:, 2 * qk + vw:])
    col_spec = pl.BlockSpec((None, None, dk, 1), lambda b, h: (b, h, 0, 0))
    row_spec = pl.BlockSpec((None, None, 1, dv), lambda b, h: (b, h, 0, 0))
    st_spec = pl.BlockSpec((None, None, dk, dv), lambda b, h: (b, h, 0, 0))
    og, s = pl.pallas_call(
        functools.partial(_gla_step_kernel, q_scale=dk ** -0.5),
        grid=(batch, GLA_HEADS),
        in_specs=[col_spec, col_spec, col_spec, row_spec, row_spec,
                  pl.BlockSpec((1, dv), lambda b, h: (0, 0)), st_spec],
        out_specs=[row_spec, st_spec],
        out_shape=[jax.ShapeDtypeStruct((batch, GLA_HEADS, 1, dv), F32),
                   jax.ShapeDtypeStruct(s0.shape, F32)],
        compiler_params=_params("parallel", "parallel"),
        name="gla_step",
    )(q_c, k_c, g_c, v_r, r_r, head_norm.reshape(1, dv), s0)
    return og.reshape(batch, vw), s


def _matmul_residual_kernel(a_ref, w_ref, h_ref, o_ref):
    o_ref[...] = h_ref[...] + jnp.dot(a_ref[...].astype(BF16), w_ref[...],
                                      preferred_element_type=F32)


def matmul_residual(a, w, h, *, tm=512, tn=512):
    m, kd = a.shape
    n = w.shape[1]
    tm = _row_tile(m, tm)
    tn = min(tn, n)
    assert n % tn == 0
    return pl.pallas_call(
        _matmul_residual_kernel,
        grid=(m // tm, n // tn),
        in_specs=[
            pl.BlockSpec((tm, kd), lambda i, j: (i, 0)),
            pl.BlockSpec((kd, tn), lambda i, j: (0, j)),
            pl.BlockSpec((tm, tn), lambda i, j: (i, j)),
        ],
        out_specs=pl.BlockSpec((tm, tn), lambda i, j: (i, j)),
        out_shape=jax.ShapeDtypeStruct((m, n), F32),
        compiler_params=_params("parallel", "parallel"),
        name="matmul_residual",
    )(a, w, h)


def _mlp_kernel(h_ref, gain_ref, wu_ref, wd_ref, o_ref, xn_ref):
    @pl.when(pl.program_id(1) == 0)
    def _():
        h = h_ref[...]
        xn_ref[...] = _rms_normalize(h, gain_ref[...]).astype(xn_ref.dtype)
        o_ref[...] = h

    u = jnp.maximum(jnp.dot(xn_ref[...], wu_ref[...], preferred_element_type=F32), 0.0)
    o_ref[...] += jnp.dot((u * u).astype(BF16), wd_ref[...], preferred_element_type=F32)


def mlp_residual(h, gain, w_up, w_down, *, tm=512, tf=512):
    m, d = h.shape
    ff = w_up.shape[1]
    tm = _row_tile(m, tm)
    assert ff % tf == 0
    return pl.pallas_call(
        _mlp_kernel,
        grid=(m // tm, ff // tf),
        in_specs=[
            pl.BlockSpec((tm, d), lambda i, f: (i, 0)),
            pl.BlockSpec((1, d), lambda i, f: (0, 0)),
            pl.BlockSpec((d, tf), lambda i, f: (0, f)),
            pl.BlockSpec((tf, d), lambda i, f: (f, 0)),
        ],
        out_specs=pl.BlockSpec((tm, d), lambda i, f: (i, 0)),
        out_shape=jax.ShapeDtypeStruct((m, d), F32),
        scratch_shapes=[pltpu.VMEM((tm, d), BF16)],
        compiler_params=_params("parallel", "arbitrary"),
        name="mlp_residual",
    )(h, gain.reshape(1, d), w_up, w_down)


def _rmsnorm_kernel(x_ref, gain_ref, o_ref):
    o_ref[...] = _rms_normalize(x_ref[...], gain_ref[...])


def rmsnorm(x, gain, *, tm=512):
    m, d = x.shape
    tm = _row_tile(m, tm)
    return pl.pallas_call(
        _rmsnorm_kernel,
        grid=(m // tm,),
        in_specs=[pl.BlockSpec((tm, d), lambda i: (i, 0)),
                  pl.BlockSpec((1, d), lambda i: (0, 0))],
        out_specs=pl.BlockSpec((tm, d), lambda i: (i, 0)),
        out_shape=jax.ShapeDtypeStruct((m, d), F32),
        compiler_params=_params("parallel"),
        name="rmsnorm",
    )(x, gain.reshape(1, d))


def _top_blocks(gate, n_past_mask, axis):
    n_blocks = gate.shape[axis]
    idx = lax.broadcasted_iota(jnp.int32, gate.shape, axis)
    neg_inf = jnp.float32(-jnp.inf)
    left = jnp.where(n_past_mask, gate, neg_inf)
    chosen = jnp.zeros(gate.shape, dtype=jnp.bool_)
    picks = []
    for _ in range(MOBA_TOP_K):
        best = jnp.max(left, axis=axis, keepdims=True)
        is_best = (left == best) & (best > neg_inf)
        first = jnp.min(jnp.where(is_best, idx, n_blocks), axis=axis, keepdims=True)
        pick = idx == first
        chosen = chosen | pick
        left = jnp.where(pick, neg_inf, left)
        picks.append(first)
    return chosen, picks


def _moba_prompt_kernel(q_ref, k_ref, v_ref, o_ref, kb_ref, vb_ref, kmean_ref,
                        *, n_blocks, scale):
    qi = pl.program_id(2)
    blk = MOBA_BLOCK

    @pl.when(qi == 0)
    def _():
        kb_ref[...] = k_ref[...].astype(BF16)
        vb_ref[...] = v_ref[...].astype(BF16)
        for n in range(n_blocks):
            kmean_ref[n:n + 1, :] = jnp.mean(k_ref[n * blk:(n + 1) * blk, :], axis=0,
                                             keepdims=True)

    q = q_ref[...]
    gate = lax.dot_general(q, kmean_ref[...], (((1,), (1,)), ((), ())),
                           precision=HIGHEST, preferred_element_type=F32)
    block_id = lax.broadcasted_iota(jnp.int32, gate.shape, 1)
    chosen, _ = _top_blocks(gate, block_id < qi, axis=1)
    chosen_f = jnp.where(chosen, 1.0, 0.0)
    q_b = q.astype(BF16)
    row = lax.broadcasted_iota(jnp.int32, (blk, blk), 0)
    col = lax.broadcasted_iota(jnp.int32, (blk, blk), 1)

    def scores(n):
        rows = pl.ds(pl.multiple_of(n * blk, blk), blk)
        return _contract_last(q_b, kb_ref[rows, :]) * scale, vb_ref[rows, :]

    s, v_blk = scores(qi)
    keep = row >= col
    m0 = jnp.max(jnp.where(keep, s, -jnp.inf), axis=-1, keepdims=True)
    p = jnp.where(keep, jnp.exp(s - m0), 0.0)
    l0 = jnp.sum(p, axis=-1, keepdims=True)
    acc0 = jnp.dot(p.astype(BF16), v_blk, preferred_element_type=F32)

    def past_block(n, carry):
        m_run, l_run, acc = carry
        s, v_blk = scores(n)
        keep = jnp.max(jnp.where(block_id == n, chosen_f, 0.0), axis=-1, keepdims=True) > 0.0
        m_new = jnp.maximum(m_run, jnp.max(jnp.where(keep, s, -jnp.inf), axis=-1, keepdims=True))
        alpha = jnp.exp(m_run - m_new)
        p = jnp.where(keep, jnp.exp(s - m_new), 0.0)
        l_new = alpha * l_run + jnp.sum(p, axis=-1, keepdims=True)
        acc = alpha * acc + jnp.dot(p.astype(BF16), v_blk, preferred_element_type=F32)
        return m_new, l_new, acc

    _, l_fin, acc = lax.fori_loop(0, qi, past_block, (m0, l0, acc0))
    o_ref[...] = (acc / l_fin).astype(o_ref.dtype)


def moba_prompt(q, k, v, *, batch, seq):
    m, width = q.shape
    hd = width // MOBA_HEADS
    assert seq % MOBA_BLOCK == 0
    n_blocks = seq // MOBA_BLOCK
    kern = functools.partial(_moba_prompt_kernel, n_blocks=n_blocks, scale=hd ** -0.5)
    return pl.pallas_call(
        kern,
        grid=(batch, MOBA_HEADS, n_blocks),
        in_specs=[
            pl.BlockSpec((MOBA_BLOCK, hd), lambda b, h, i: (b * n_blocks + i, h)),
            pl.BlockSpec((seq, hd), lambda b, h, i: (b, h)),
            pl.BlockSpec((seq, hd), lambda b, h, i: (b, h)),
        ],
        out_specs=pl.BlockSpec((MOBA_BLOCK, hd), lambda b, h, i: (b * n_blocks + i, h)),
        out_shape=jax.ShapeDtypeStruct((m, width), BF16),
        scratch_shapes=[pltpu.VMEM((seq, hd), BF16), pltpu.VMEM((seq, hd), BF16),
                        pltpu.VMEM((n_blocks, hd), F32)],
        compiler_params=_params("parallel", "parallel", "arbitrary"),
        name="moba_prompt",
    )(q, k, v)


def _paged_kmean_kernel(pt_ref, *refs, pages_per_block, block_len):
    page_refs, o_ref = refs[:pages_per_block], refs[pages_per_block]
    n = pl.program_id(1)
    total = jnp.sum(page_refs[0][...], axis=0, keepdims=True)
    for page_ref in page_refs[1:]:
        total = total + jnp.sum(page_ref[...], axis=0, keepdims=True)
    o_ref[pl.ds(n, 1), :] = total / block_len


def paged_kmean(cache_k, page_table):
    _, page, width = cache_k.shape
    batch, n_pages = page_table.shape
    ppb = MOBA_BLOCK // page
    assert MOBA_BLOCK % page == 0 and n_pages % ppb == 0
    n_past = n_pages // ppb

    def page_spec(slot):
        return pl.BlockSpec((None, page, width),
                            lambda b, n, pt: (pt[b, n * ppb + slot], 0, 0))

    return pl.pallas_call(
        functools.partial(_paged_kmean_kernel, pages_per_block=ppb, block_len=MOBA_BLOCK),
        grid_spec=pltpu.PrefetchScalarGridSpec(
            num_scalar_prefetch=1,
            grid=(batch, n_past),
            in_specs=[page_spec(s) for s in range(ppb)],
            out_specs=pl.BlockSpec((None, n_past, width), lambda b, n, pt: (b, 0, 0)),
        ),
        out_shape=jax.ShapeDtypeStruct((batch, n_past, width), F32),
        compiler_params=_params("parallel", "arbitrary"),
        name="paged_kmean",
    )(page_table, *([cache_k] * ppb))


def _decode_select_kernel(q_ref, kmean_ref, o_ref):
    n_past, width = kmean_ref.shape
    hd = width // MOBA_HEADS
    prod = kmean_ref[...] * q_ref[...]
    head_id = lax.broadcasted_iota(jnp.int32, (n_past, MOBA_HEADS), 1)
    gate = jnp.zeros((n_past, MOBA_HEADS), F32)
    for h in range(MOBA_HEADS):
        part = jnp.sum(prod[:, h * hd:(h + 1) * hd], axis=-1, keepdims=True)
        gate = jnp.where(head_id == h, part, gate)
    _, picks = _top_blocks(gate, jnp.full(gate.shape, True), axis=0)
    for r, first in enumerate(picks):
        o_ref[r:r + 1, :] = first


def decode_select(q, kmean):
    batch, n_past, width = kmean.shape
    assert n_past >= MOBA_TOP_K
    return pl.pallas_call(
        _decode_select_kernel,
        grid=(batch,),
        in_specs=[pl.BlockSpec((None, 1, width), lambda b: (b, 0, 0)),
                  pl.BlockSpec((None, n_past, width), lambda b: (b, 0, 0))],
        out_specs=pl.BlockSpec((None, MOBA_TOP_K, MOBA_HEADS), lambda b: (b, 0, 0)),
        out_shape=jax.ShapeDtypeStruct((batch, MOBA_TOP_K, MOBA_HEADS), jnp.int32),
        compiler_params=_params("parallel"),
        name="moba_decode_select",
    )(q, kmean)


def _decode_attend_kernel(pt_ref, top_ref, q_ref, kn_ref, vn_ref, *refs, n_pages, scale):
    k_refs, v_refs, o_ref = refs[:n_pages], refs[n_pages:2 * n_pages], refs[2 * n_pages]
    q = q_ref[...]
    s_own = jnp.sum(q * kn_ref[...], axis=-1, keepdims=True) * scale
    s_past = [jnp.sum(k_ref[...] * q, axis=-1, keepdims=True) * scale for k_ref in k_refs]
    m = s_own
    for s in s_past:
        m = jnp.maximum(m, jnp.max(s, axis=0, keepdims=True))
    p_own = jnp.exp(s_own - m)
    denom = p_own
    acc = p_own * vn_ref[...]
    for s, v_ref in zip(s_past, v_refs):
        p = jnp.exp(s - m)
        denom = denom + jnp.sum(p, axis=0, keepdims=True)
        acc = acc + jnp.sum(p * v_ref[...], axis=0, keepdims=True)
    o_ref[...] = acc / denom


def decode_attend(q, k_new, v_new, cache_k, cache_v, page_table, top):
    batch, _, width = q.shape
    page = cache_k.shape[1]
    hd = width // MOBA_HEADS
    ppb = MOBA_BLOCK // page
    n_pages = MOBA_TOP_K * ppb

    tok_spec = pl.BlockSpec((None, 1, hd), lambda b, h, pt, tp: (b, 0, h))

    def page_spec(slot):
        r, pg = divmod(slot, ppb)
        return pl.BlockSpec((None, page, hd),
                            lambda b, h, pt, tp: (pt[b, tp[b, r, h] * ppb + pg], 0, h))

    page_specs = [page_spec(s) for s in range(n_pages)]
    return pl.pallas_call(
        functools.partial(_decode_attend_kernel, n_pages=n_pages, scale=hd ** -0.5),
        grid_spec=pltpu.PrefetchScalarGridSpec(
            num_scalar_prefetch=2,
            grid=(batch, MOBA_HEADS),
            in_specs=[tok_spec, tok_spec, tok_spec] + page_specs + page_specs,
            out_specs=tok_spec,
        ),
        out_shape=jax.ShapeDtypeStruct((batch, 1, width), F32),
        compiler_params=_params("parallel", "parallel"),
        name="moba_decode_attend",
    )(page_table, top, q, k_new, v_new, *([cache_k] * n_pages), *([cache_v] * n_pages))


def _trunk(x, weights, *, batch, seq, gla_s0=None, cache=None):
    w = weights
    n_gla = w["w_qkvr"].shape[0]
    depth = w["w_up"].shape[0]
    width = w["w_kv"].shape[1] // 2
    h = x
    states = []
    k_new = v_new = kmean = None
    for l in range(depth):
        if l < n_gla:
            proj = norm_matmul(h, w["norm_mix"][l], w["w_qkvr"][l])[0]
            g = gla_gate(h, w["norm_mix"][l], w["w_z"][l], w["w_gate"][l], w["b_gate"][l])
            if cache is None:
                og, s = gla_prompt(proj, g, w["head_norm"][l], batch=batch, seq=seq)
            else:
                og, s = gla_step(proj, g, w["head_norm"][l], gla_s0[l])
            states.append(s)
            h = matmul_residual(og, w["w_out"][l], h)
        else:
            i = l - n_gla
            q = norm_matmul(h, w["norm_mix"][l], w["w_q"][i])[0]
            if cache is None:
                o = moba_prompt(q, k_new, v_new, batch=batch, seq=seq)
            else:
                cache_k, cache_v, page_table = cache
                q3 = q.reshape(batch, 1, width)
                top = decode_select(q3, kmean)
                o = decode_attend(q3, k_new.reshape(batch, 1, width),
                                  v_new.reshape(batch, 1, width), cache_k, cache_v,
                                  page_table, top).reshape(batch, width)
            h = matmul_residual(o, w["w_o"][i], h)
        h = mlp_residual(h, w["norm_mlp"][l], w["w_up"][l], w["w_down"][l])
        if l == n_gla - 1:
            kv = norm_matmul(h, w["kv_norm"], w["w_kv"], n_split=2)
            k_new, v_new = kv[0], kv[1]
            if cache is not None:
                kmean = paged_kmean(cache[0], cache[2])
    y = rmsnorm(h, w["norm_final"])
    return y, jnp.stack(states), k_new, v_new


def kernel(x_prompt, x_sample, state_gla, cache_k, cache_v, page_table, norm_mix, norm_mlp,
           w_mlp_up, w_mlp_down, w_in_a, w_gate_a, b_gate_a, head_norm_a, w_out_a, kv_norm,
           w_kv, w_q_b, w_o_b, norm_final):
    batch, seq, d = x_prompt.shape
    dec_batch, dec_seq, _ = x_sample.shape
    assert dec_seq == 1
    n_qkvr = w_in_a.shape[2] - GLA_GATE_RANK
    weights = dict(
        norm_mix=norm_mix, norm_mlp=norm_mlp, norm_final=norm_final, kv_norm=kv_norm,
        head_norm=head_norm_a, w_gate=w_gate_a, b_gate=b_gate_a,
        w_qkvr=w_in_a[:, :, :n_qkvr].astype(BF16),
        w_z=w_in_a[:, :, n_qkvr:],
        w_out=w_out_a.astype(BF16),
        w_up=w_mlp_up.astype(BF16),
        w_down=w_mlp_down.astype(BF16),
        w_kv=w_kv.astype(BF16),
        w_q=w_q_b.astype(BF16),
        w_o=w_o_b.astype(BF16),
    )
    heads, hd = cache_k.shape[2], cache_k.shape[3]
    n_phys, page = cache_k.shape[0], cache_k.shape[1]

    y_p, s_p, k_p, v_p = _trunk(x_prompt.reshape(batch * seq, d), weights, batch=batch, seq=seq)
    cache = (cache_k.reshape(n_phys, page, heads * hd), cache_v.reshape(n_phys, page, heads * hd),
             page_table)
    y_s, s_s, k_s, v_s = _trunk(x_sample.reshape(dec_batch, d), weights, batch=dec_batch, seq=1,
                                gla_s0=state_gla, cache=cache)
    return (y_p.reshape(batch, seq, d), y_s.reshape(dec_batch, 1, d), s_p, s_s,
            k_p.reshape(batch, seq, heads, hd), v_p.reshape(batch, seq, heads, hd),
            k_s.reshape(dec_batch, 1, heads, hd), v_s.reshape(dec_batch, 1, heads, hd))
```

```python
import functools

import jax
import jax.numpy as jnp
from jax import lax
from jax.experimental import pallas as pl
from jax.experimental.pallas import tpu as pltpu

F32 = jnp.float32
BF16 = jnp.bfloat16
HIGHEST = lax.Precision.HIGHEST

RMS_EPS = 1e-6
GLA_HEADS = 4
GLA_GATE_RANK = 16
GLA_GATE_TAU = 16.0
GLA_CHUNK = 64
GLA_SUB = 16
MOBA_HEADS = 16
MOBA_BLOCK = 256
MOBA_TOP_K = 3
LOG2_E = 1.4426950408889634

V7X_VMEM_BYTES = 64 * 1024 * 1024
VMEM_LIMIT_BYTES = V7X_VMEM_BYTES - 8 * 1024 * 1024
V7X_SUBLANES = 8


def _params(*semantics):
    return pltpu.CompilerParams(dimension_semantics=semantics,
                                vmem_limit_bytes=VMEM_LIMIT_BYTES)


def _rms_normalize(x, gain):
    ms = jnp.mean(x * x, axis=-1, keepdims=True)
    return x * lax.rsqrt(ms + RMS_EPS) * gain


def _row_tile(m, want):
    return want if m % want == 0 else m


def _col_tile(n, want):
    while n % want:
        want //= 2
    return want


def _contract_last(a, b, **kw):
    return lax.dot_general(a, b, (((1,), (1,)), ((), ())), preferred_element_type=F32, **kw)


def _contract_first(a, b):
    return lax.dot_general(a, b, (((0,), (0,)), ((), ())), preferred_element_type=F32)


def _norm_matmul_kernel(x_ref, gain_ref, *refs, n_split):
    w_refs, o_refs, xn_ref = refs[:n_split], refs[n_split:2 * n_split], refs[2 * n_split]

    @pl.when(pl.program_id(1) == 0)
    def _():
        xn_ref[...] = _rms_normalize(x_ref[...], gain_ref[...]).astype(xn_ref.dtype)

    for w_ref, o_ref in zip(w_refs, o_refs):
        o_ref[...] = jnp.dot(xn_ref[...], w_ref[...], preferred_element_type=F32)


def norm_matmul(x, gain, w, layer, *, n_split=1, tm=512, tn=1024):
    m, d = x.shape
    n = w.shape[2]
    n_out = n // n_split
    tm = _row_tile(m, tm)
    tn = _col_tile(n_out, tn)
    per_split = n_out // tn

    def w_spec(s):
        return pl.BlockSpec((None, d, tn), lambda i, j: (layer, 0, s * per_split + j))

    return pl.pallas_call(
        functools.partial(_norm_matmul_kernel, n_split=n_split),
        grid=(m // tm, per_split),
        in_specs=[pl.BlockSpec((tm, d), lambda i, j: (i, 0)),
                  pl.BlockSpec((1, d), lambda i, j: (0, 0))]
                 + [w_spec(s) for s in range(n_split)],
        out_specs=[pl.BlockSpec((tm, tn), lambda i, j: (i, j))] * n_split,
        out_shape=[jax.ShapeDtypeStruct((m, n_out), F32)] * n_split,
        scratch_shapes=[pltpu.VMEM((tm, d), BF16)],
        compiler_params=_params("parallel", "arbitrary"),
        name="norm_matmul",
    )(x, gain.reshape(1, d), *([w] * n_split))


def _split_bf16(a):
    hi = a.astype(BF16)
    return hi, a - hi.astype(F32)


def _dot_split(a, b):
    a_hi, a_rest = _split_bf16(a)
    b_hi, b_rest = _split_bf16(b)
    a_lo, b_lo = a_rest.astype(BF16), b_rest.astype(BF16)
    return (jnp.dot(a_hi, b_hi, preferred_element_type=F32)
            + jnp.dot(a_lo, b_hi, preferred_element_type=F32)
            + jnp.dot(a_hi, b_lo, preferred_element_type=F32))


def _gla_gate_kernel(x_ref, gain_ref, wz_ref, wg_ref, bg_ref, o_ref, *, chunk):
    xn = _rms_normalize(x_ref[...], gain_ref[...])
    z = _dot_split(xn, wz_ref[...])
    logit = _dot_split(z, wg_ref[...]) + bg_ref[...]
    log_sig = jnp.minimum(logit, 0.0) - jnp.log(1.0 + jnp.exp(-jnp.abs(logit)))
    g = log_sig / GLA_GATE_TAU
    if chunk > 1:
        rows = g.shape[0]
        row = lax.broadcasted_iota(jnp.int32, (rows, rows), 0)
        col = lax.broadcasted_iota(jnp.int32, (rows, rows), 1)
        same_chunk_past = (row >= col) & (row // chunk == col // chunk)
        ones = jnp.where(same_chunk_past, 1.0, 0.0).astype(BF16)
        g_hi, g_rest = _split_bf16(g)
        g_mid, g_rest = _split_bf16(g_rest)
        g = (jnp.dot(ones, g_hi, preferred_element_type=F32)
             + jnp.dot(ones, g_mid, preferred_element_type=F32)
             + jnp.dot(ones, g_rest.astype(BF16), preferred_element_type=F32))
    o_ref[...] = g


def gla_gate(x, gain, w_z, w_gate, b_gate, layer, *, chunk, tm=256):
    m, d = x.shape
    rank, n = w_gate.shape[1:]
    tm = _row_tile(m, tm)
    assert chunk == 1 or tm % chunk == 0
    return pl.pallas_call(
        functools.partial(_gla_gate_kernel, chunk=chunk),
        grid=(m // tm,),
        in_specs=[
            pl.BlockSpec((tm, d), lambda i: (i, 0)),
            pl.BlockSpec((1, d), lambda i: (0, 0)),
            pl.BlockSpec((None, d, rank), lambda i: (layer, 0, 0)),
            pl.BlockSpec((None, rank, n), lambda i: (layer, 0, 0)),
            pl.BlockSpec((None, 1, n), lambda i: (layer, 0, 0)),
        ],
        out_specs=pl.BlockSpec((tm, n), lambda i: (i, 0)),
        out_shape=jax.ShapeDtypeStruct((m, n), F32),
        compiler_params=_params("parallel"),
        name="gla_gate",
    )(x, gain.reshape(1, d), w_z, w_gate, b_gate.reshape(b_gate.shape[0], 1, n))


def _gla_chunk(q, k, v, b, st):
    c, dk = q.shape
    n_sub = c // GLA_SUB
    row = lax.broadcasted_iota(jnp.int32, (c, c), 0)
    col = lax.broadcasted_iota(jnp.int32, (c, c), 1)

    def rows_of(r, n):
        return jnp.concatenate(
            [jnp.broadcast_to(b[j * GLA_SUB + r:j * GLA_SUB + r + 1, :], (GLA_SUB, dk))
             for j in range(n)], axis=0)

    b_end = b[c - 1:c, :]
    v_b = v.astype(BF16)

    k_end = (k * jnp.exp(b_end - b)).astype(BF16)
    st_new = st * jnp.exp(b_end) + _contract_first(v_b, k_end)

    from_start = b - rows_of(0, n_sub)
    q_diag = q * jnp.exp(from_start)
    k_diag = k * jnp.exp(-from_start)
    att_diag = _contract_last(q_diag.astype(BF16), k_diag.astype(BF16))
    same_sub = (row // GLA_SUB) == (col // GLA_SUB)
    att = jnp.where(same_sub & (row >= col), att_diag, 0.0)

    if n_sub > 1:
        past = c - GLA_SUB
        k_off = (k[:past] * jnp.exp(rows_of(GLA_SUB - 1, n_sub - 1) - b[:past])).astype(BF16)
        q_parts, k_parts = [], []
        for j in range(n_sub - 1):
            lo = (j + 1) * GLA_SUB
            q_j = (q[lo:] * jnp.exp(b[lo:] - b[lo - 1:lo, :])).astype(BF16)
            q_parts.append(jnp.concatenate([jnp.zeros((lo, dk), BF16), q_j], axis=0))
            pieces = [k_off[lo - GLA_SUB:lo], jnp.zeros((c - lo, dk), BF16)]
            if j:
                pieces.insert(0, jnp.zeros((lo - GLA_SUB, dk), BF16))
            k_parts.append(jnp.concatenate(pieces, axis=0))
        att = att + _contract_last(jnp.concatenate(q_parts, axis=1),
                                   jnp.concatenate(k_parts, axis=1))

    o = _contract_last((q * jnp.exp(b)).astype(BF16), st.astype(BF16))
    o = o + jnp.dot(att.astype(BF16), v_b, preferred_element_type=F32)
    return o, st_new


def _gla_prompt_kernel(q_ref, k_ref, v_ref, r_ref, b_ref, hn_ref, og_ref, s_ref, st_ref,
                       *, n_chunks, q_scale, unroll):
    t = pl.program_id(2)

    @pl.when(t == 0)
    def _():
        st_ref[...] = jnp.zeros_like(st_ref)

    def chunk(ci, carry):
        rows = pl.ds(pl.multiple_of(ci * GLA_CHUNK, GLA_CHUNK), GLA_CHUNK)
        o, st_new = _gla_chunk(q_ref[rows, :] * q_scale, k_ref[rows, :], v_ref[rows, :],
                               b_ref[rows, :], st_ref[...])
        st_ref[...] = st_new
        r = r_ref[rows, :]
        silu = r / (1.0 + jnp.exp(-r))
        og_ref[rows, :] = (_rms_normalize(o, hn_ref[...]) * silu).astype(og_ref.dtype)
        return carry

    lax.fori_loop(0, n_chunks, chunk, 0, unroll=unroll)

    @pl.when(t == pl.num_programs(2) - 1)
    def _():
        s_ref[...] = st_ref[...].T


def gla_prompt(proj, b, head_norm, layer, *, batch, seq, tt=512, unroll=4):
    m = batch * seq
    qk = b.shape[1]
    dk = qk // GLA_HEADS
    vw = (proj.shape[1] - 2 * qk) // 2
    dv = vw // GLA_HEADS
    tt = min(tt, seq)
    assert seq % tt == 0 and tt % GLA_CHUNK == 0 and dv % dk == 0
    nt = seq // tt
    k_col0 = qk // dk
    v_col0 = 2 * qk // dv
    r_col0 = (2 * qk + vw) // dv
    kern = functools.partial(_gla_prompt_kernel, n_chunks=tt // GLA_CHUNK, q_scale=dk ** -0.5,
                             unroll=unroll)
    return pl.pallas_call(
        kern,
        grid=(batch, GLA_HEADS, nt),
        in_specs=[
            pl.BlockSpec((tt, dk), lambda bi, h, t: (bi * nt + t, h)),
            pl.BlockSpec((tt, dk), lambda bi, h, t: (bi * nt + t, k_col0 + h)),
            pl.BlockSpec((tt, dv), lambda bi, h, t: (bi * nt + t, v_col0 + h)),
            pl.BlockSpec((tt, dv), lambda bi, h, t: (bi * nt + t, r_col0 + h)),
            pl.BlockSpec((tt, dk), lambda bi, h, t: (bi * nt + t, h)),
            pl.BlockSpec((None, 1, dv), lambda bi, h, t: (layer, 0, 0)),
        ],
        out_specs=[
            pl.BlockSpec((tt, dv), lambda bi, h, t: (bi * nt + t, h)),
            pl.BlockSpec((None, None, dk, dv), lambda bi, h, t: (bi, h, 0, 0)),
        ],
        out_shape=[
            jax.ShapeDtypeStruct((m, vw), BF16),
            jax.ShapeDtypeStruct((batch, GLA_HEADS, dk, dv), F32),
        ],
        scratch_shapes=[pltpu.VMEM((dv, dk), F32)],
        compiler_params=_params("parallel", "parallel", "arbitrary"),
        name="gla_prompt",
    )(proj, proj, proj, proj, b, head_norm.reshape(head_norm.shape[0], 1, dv))


def _gla_step_kernel(q_ref, k_ref, g_ref, v_ref, r_ref, hn_ref, s0_ref, og_ref, s_ref,
                     *, q_scale):
    s_new = jnp.exp(g_ref[...]) * s0_ref[...] + k_ref[...] * v_ref[...]
    s_ref[...] = s_new
    o = jnp.sum((q_ref[...] * q_scale) * s_new, axis=0, keepdims=True)
    r = r_ref[...]
    silu = r / (1.0 + jnp.exp(-r))
    og_ref[...] = _rms_normalize(o, hn_ref[...]) * silu


def gla_step(proj, g, head_norm, s0, layer):
    batch = proj.shape[0]
    qk = g.shape[1]
    dk = qk // GLA_HEADS
    vw = (proj.shape[1] - 2 * qk) // 2
    dv = vw // GLA_HEADS
    col = lambda a: a.reshape(batch, GLA_HEADS, dk, 1)
    row = lambda a: a.reshape(batch, GLA_HEADS, 1, dv)
    q_c = col(proj[:, :qk])
    k_c = col(proj[:, qk:2 * qk])
    g_c = col(g)
    v_r = row(proj[:, 2 * qk:2 * qk + vw])
    r_r = row(proj[:, 2 * qk + vw:])
    col_spec = pl.BlockSpec((None, None, dk, 1), lambda bi, h: (bi, h, 0, 0))
    row_spec = pl.BlockSpec((None, None, 1, dv), lambda bi, h: (bi, h, 0, 0))
    st_spec = pl.BlockSpec((None, None, dk, dv), lambda bi, h: (bi, h, 0, 0))
    og, s = pl.pallas_call(
        functools.partial(_gla_step_kernel, q_scale=dk ** -0.5),
        grid=(batch, GLA_HEADS),
        in_specs=[col_spec, col_spec, col_spec, row_spec, row_spec,
                  pl.BlockSpec((None, 1, dv), lambda bi, h: (layer, 0, 0)),
                  pl.BlockSpec((None, None, None, dk, dv), lambda bi, h: (layer, bi, h, 0, 0))],
        out_specs=[row_spec, st_spec],
        out_shape=[jax.ShapeDtypeStruct((batch, GLA_HEADS, 1, dv), F32),
                   jax.ShapeDtypeStruct(s0.shape[1:], F32)],
        compiler_params=_params("parallel", "parallel"),
        name="gla_step",
    )(q_c, k_c, g_c, v_r, r_r, head_norm.reshape(head_norm.shape[0], 1, dv), s0)
    return og.reshape(batch, vw), s


def _matmul_residual_kernel(a_ref, w_ref, h_ref, o_ref):
    o_ref[...] = h_ref[...] + jnp.dot(a_ref[...].astype(BF16), w_ref[...],
                                      preferred_element_type=F32)


def matmul_residual(a, w, h, layer, *, tm=512, tn=2048):
    m, kd = a.shape
    n = w.shape[2]
    tm = _row_tile(m, tm)
    tn = _col_tile(n, tn)
    return pl.pallas_call(
        _matmul_residual_kernel,
        grid=(m // tm, n // tn),
        in_specs=[
            pl.BlockSpec((tm, kd), lambda i, j: (i, 0)),
            pl.BlockSpec((None, kd, tn), lambda i, j: (layer, 0, j)),
            pl.BlockSpec((tm, tn), lambda i, j: (i, j)),
        ],
        out_specs=pl.BlockSpec((tm, tn), lambda i, j: (i, j)),
        out_shape=jax.ShapeDtypeStruct((m, n), F32),
        compiler_params=_params("parallel", "parallel"),
        name="matmul_residual",
    )(a, w, h)


def _mlp_kernel(h_ref, gain_ref, wu_ref, wd_ref, o_ref, xn_ref):
    @pl.when(pl.program_id(1) == 0)
    def _():
        h = h_ref[...]
        xn_ref[...] = _rms_normalize(h, gain_ref[...]).astype(xn_ref.dtype)
        o_ref[...] = h

    u = jnp.maximum(jnp.dot(xn_ref[...], wu_ref[...], preferred_element_type=F32), 0.0)
    o_ref[...] += jnp.dot((u * u).astype(BF16), wd_ref[...], preferred_element_type=F32)


def mlp_residual(h, gain, w_up, w_down, layer, *, tm=512, tf=1024):
    m, d = h.shape
    ff = w_up.shape[2]
    tm = _row_tile(m, tm)
    tf = _col_tile(ff, tf)
    return pl.pallas_call(
        _mlp_kernel,
        grid=(m // tm, ff // tf),
        in_specs=[
            pl.BlockSpec((tm, d), lambda i, f: (i, 0)),
            pl.BlockSpec((None, 1, d), lambda i, f: (layer, 0, 0)),
            pl.BlockSpec((None, d, tf), lambda i, f: (layer, 0, f)),
            pl.BlockSpec((None, tf, d), lambda i, f: (layer, f, 0)),
        ],
        out_specs=pl.BlockSpec((tm, d), lambda i, f: (i, 0)),
        out_shape=jax.ShapeDtypeStruct((m, d), F32),
        scratch_shapes=[pltpu.VMEM((tm, d), BF16)],
        compiler_params=_params("parallel", "arbitrary"),
        name="mlp_residual",
    )(h, gain.reshape(gain.shape[0], 1, d), w_up, w_down)


def _rmsnorm_kernel(x_ref, gain_ref, o_ref):
    o_ref[...] = _rms_normalize(x_ref[...], gain_ref[...])


def rmsnorm(x, gain, *, tm=512):
    m, d = x.shape
    tm = _row_tile(m, tm)
    return pl.pallas_call(
        _rmsnorm_kernel,
        grid=(m // tm,),
        in_specs=[pl.BlockSpec((tm, d), lambda i: (i, 0)),
                  pl.BlockSpec((1, d), lambda i: (0, 0))],
        out_specs=pl.BlockSpec((tm, d), lambda i: (i, 0)),
        out_shape=jax.ShapeDtypeStruct((m, d), F32),
        compiler_params=_params("parallel"),
        name="rmsnorm",
    )(x, gain.reshape(1, d))


def _top_blocks(gate, is_past, axis):
    n_blocks = gate.shape[axis]
    idx = lax.broadcasted_iota(jnp.int32, gate.shape, axis)
    neg_inf = jnp.float32(-jnp.inf)
    left = jnp.where(is_past, gate, neg_inf)
    chosen = jnp.zeros(gate.shape, dtype=jnp.bool_)
    picks = []
    for _ in range(MOBA_TOP_K):
        best = jnp.max(left, axis=axis, keepdims=True)
        is_best = (left == best) & (best > neg_inf)
        first = jnp.min(jnp.where(is_best, idx, n_blocks), axis=axis, keepdims=True)
        pick = idx == first
        chosen = chosen | pick
        left = jnp.where(pick, neg_inf, left)
        picks.append(first)
    return chosen, picks


def _moba_prompt_kernel(q_ref, k_ref, v_ref, o_ref, kb_ref, vt_ref, kmean_ref,
                        *, n_blocks, n_heads, hd, scale, tq):
    ti = pl.program_id(2)
    blk = MOBA_BLOCK
    heads = range(n_heads)

    @pl.when(ti == 0)
    def _():
        for j in heads:
            cols = slice(j * hd, (j + 1) * hd)
            for n in range(n_blocks):
                k_blk = k_ref[n * blk:(n + 1) * blk, cols]
                kb_ref[j, n] = k_blk.astype(BF16)
                vt_ref[j, n] = v_ref[n * blk:(n + 1) * blk, cols].T.astype(BF16)
                kmean_ref[j, n:n + 1, :] = jnp.mean(k_blk, axis=0, keepdims=True)

    q_pos = ti * tq + lax.broadcasted_iota(jnp.int32, (1, tq), 1)
    q_block = q_pos // blk
    block_id = lax.broadcasted_iota(jnp.int32, (n_blocks, tq), 0)
    key_off = lax.broadcasted_iota(jnp.int32, (blk, tq), 0)
    neg_inf = jnp.float32(-jnp.inf)

    q_bs, visible, carry0 = [], [], []
    for j in heads:
        q = q_ref[:, j * hd:(j + 1) * hd]
        gate = _contract_last(kmean_ref[j], q, precision=HIGHEST)
        chosen, _ = _top_blocks(gate, block_id < q_block, axis=0)
        visible.append(jnp.where(chosen | (block_id == q_block), 1.0, 0.0))
        q_bs.append((q * (scale * LOG2_E)).astype(BF16))
        carry0.append((jnp.full((1, tq), neg_inf), jnp.zeros((1, tq), F32),
                       jnp.zeros((hd, tq), F32)))

    def key_block(n, carry):
        q_off = q_pos - n * blk
        raw = [_contract_last(kb_ref[j, n], q_bs[j]) for j in heads]
        partial = []
        for j in heads:
            m_run, l_run, _ = carry[j]
            vis = jnp.max(jnp.where(block_id == n, visible[j], 0.0), axis=0, keepdims=True)
            threshold = jnp.where(vis > 0.0, q_off, -1)
            s = jnp.where(key_off <= threshold, raw[j], neg_inf)
            m_new = jnp.maximum(m_run, jnp.max(s, axis=0, keepdims=True))
            seen = m_new > neg_inf
            m_safe = jnp.where(seen, m_new, 0.0)
            alpha = jnp.where(seen, jnp.exp2(m_run - m_safe), 0.0)
            p = jnp.exp2(s - m_safe)
            l_new = alpha * l_run + jnp.sum(p, axis=0, keepdims=True)
            partial.append((m_new, l_new, alpha, p.astype(BF16)))
        out = []
        for j in heads:
            m_new, l_new, alpha, p_b = partial[j]
            acc = alpha * carry[j][2] + jnp.dot(vt_ref[j, n], p_b,
                                                preferred_element_type=F32)
            out.append((m_new, l_new, acc))
        return tuple(out)

    final = lax.fori_loop(0, (ti + 1) * (tq // blk), key_block, tuple(carry0))
    for j in heads:
        _, l_fin, acc = final[j]
        o_ref[:, j * hd:(j + 1) * hd] = (acc / l_fin).T.astype(o_ref.dtype)


def moba_prompt(q, k, v, *, batch, seq, heads_per_step=4, tq=512):
    m, width = q.shape
    hd = width // MOBA_HEADS
    tq = min(tq, seq)
    assert seq % tq == 0 and tq % MOBA_BLOCK == 0 and MOBA_HEADS % heads_per_step == 0
    n_blocks = seq // MOBA_BLOCK
    n_tiles = seq // tq
    gw = heads_per_step * hd
    kern = functools.partial(_moba_prompt_kernel, n_blocks=n_blocks, n_heads=heads_per_step,
                             hd=hd, scale=hd ** -0.5, tq=tq)
    return pl.pallas_call(
        kern,
        grid=(batch, MOBA_HEADS // heads_per_step, n_tiles),
        in_specs=[
            pl.BlockSpec((tq, gw), lambda b, h, i: (b * n_tiles + i, h)),
            pl.BlockSpec((seq, gw), lambda b, h, i: (b, h)),
            pl.BlockSpec((seq, gw), lambda b, h, i: (b, h)),
        ],
        out_specs=pl.BlockSpec((tq, gw), lambda b, h, i: (b * n_tiles + i, h)),
        out_shape=jax.ShapeDtypeStruct((m, width), BF16),
        scratch_shapes=[pltpu.VMEM((heads_per_step, n_blocks, MOBA_BLOCK, hd), BF16),
                        pltpu.VMEM((heads_per_step, n_blocks, hd, MOBA_BLOCK), BF16),
                        pltpu.VMEM((heads_per_step, n_blocks, hd), F32)],
        compiler_params=_params("parallel", "parallel", "arbitrary"),
        name="moba_prompt",
    )(q, k, v)


def _paged_kmean_kernel(pt_ref, *refs, blocks_per_step, pages_per_block, block_len):
    n_in = blocks_per_step * pages_per_block
    page_refs, o_ref = refs[:n_in], refs[n_in]
    step = pl.program_id(1)
    for i in range(blocks_per_step):
        total = jnp.sum(page_refs[i * pages_per_block][...], axis=0)
        for page_ref in page_refs[i * pages_per_block + 1:(i + 1) * pages_per_block]:
            total = total + jnp.sum(page_ref[...], axis=0)
        o_ref[step * blocks_per_step + i] = total / block_len


def paged_kmean(cache_k, page_table_flat, *, batch, n_pages, blocks_per_step=2):
    _, page, heads, hd = cache_k.shape
    ppb = MOBA_BLOCK // page
    assert MOBA_BLOCK % page == 0 and n_pages % (ppb * blocks_per_step) == 0
    n_past = n_pages // ppb
    n_in = ppb * blocks_per_step

    def page_spec(slot):
        return pl.BlockSpec((None, page, heads, hd),
                            lambda b, n, pt: (pt[b * n_pages + n * n_in + slot], 0, 0, 0))

    return pl.pallas_call(
        functools.partial(_paged_kmean_kernel, blocks_per_step=blocks_per_step,
                          pages_per_block=ppb, block_len=MOBA_BLOCK),
        grid_spec=pltpu.PrefetchScalarGridSpec(
            num_scalar_prefetch=1,
            grid=(batch, n_past // blocks_per_step),
            in_specs=[page_spec(s) for s in range(n_in)],
            out_specs=pl.BlockSpec((None, n_past, heads, hd), lambda b, n, pt: (b, 0, 0, 0)),
        ),
        out_shape=jax.ShapeDtypeStruct((batch, n_past, heads, hd), F32),
        compiler_params=_params("parallel", "arbitrary"),
        name="paged_kmean",
    )(page_table_flat, *([cache_k] * n_in))


def _decode_select_kernel(q_ref, kmean_ref, o_ref):
    gate = jnp.sum(kmean_ref[...] * q_ref[...], axis=-1, keepdims=True)
    _, picks = _top_blocks(gate, jnp.full(gate.shape, True), axis=0)
    for r, first in enumerate(picks):
        o_ref[r] = first[0]


def decode_select(q, kmean):
    batch, n_past, heads, hd = kmean.shape
    assert n_past >= MOBA_TOP_K
    return pl.pallas_call(
        _decode_select_kernel,
        grid=(batch,),
        in_specs=[pl.BlockSpec((None, heads, hd), lambda b: (b, 0, 0)),
                  pl.BlockSpec((None, n_past, heads, hd), lambda b: (b, 0, 0, 0))],
        out_specs=pl.BlockSpec((None, MOBA_TOP_K, heads, 1), lambda b: (b, 0, 0, 0)),
        out_shape=jax.ShapeDtypeStruct((batch, MOBA_TOP_K, heads, 1), jnp.int32),
        compiler_params=_params("parallel"),
        name="moba_decode_select",
    )(q, kmean)


def _decode_attend_kernel(pt_ref, top_ref, q_ref, kn_ref, vn_ref, *refs, n_pages, scale):
    k_refs, v_refs, o_ref = refs[:n_pages], refs[n_pages:2 * n_pages], refs[2 * n_pages]
    sub = pl.program_id(1) % V7X_SUBLANES
    q = q_ref[...]
    s_own = jnp.sum(q * kn_ref[...], axis=-1, keepdims=True) * scale
    s_past = [jnp.sum(k_ref[...] * q, axis=-1, keepdims=True) * scale
              for k_ref in k_refs]
    m = s_own
    for s in s_past:
        m = jnp.maximum(m, jnp.max(s, axis=0))
    p_own = jnp.exp(s_own - m)
    denom = p_own
    acc = p_own * vn_ref[...]
    for s, v_ref in zip(s_past, v_refs):
        p = jnp.exp(s - m)
        denom = denom + jnp.sum(p, axis=0)
        acc = acc + jnp.sum(p * v_ref[...], axis=0)
    out = acc / denom
    mine = lax.broadcasted_iota(jnp.int32, out.shape, 0) == sub
    o_ref[pl.ds(sub, 1), :] = jnp.sum(jnp.where(mine, out, 0.0), axis=0, keepdims=True)


def decode_attend(q, k_new, v_new, cache_k, cache_v, page_table_flat, top_flat, *, n_pages_seq):
    batch, heads, hd = q.shape
    page = cache_k.shape[1]
    ppb = MOBA_BLOCK // page
    n_pages = MOBA_TOP_K * ppb
    assert heads % V7X_SUBLANES == 0

    tok_spec = pl.BlockSpec((None, V7X_SUBLANES, hd),
                            lambda b, h, pt, tp: (b, h // V7X_SUBLANES, 0))

    def page_spec(slot):
        r, pg = divmod(slot, ppb)

        def index(b, h, pt, tp):
            block = tp[(b * MOBA_TOP_K + r) * heads + h]
            return (pt[b * n_pages_seq + block * ppb + pg], 0, h // V7X_SUBLANES, 0)

        return pl.BlockSpec((None, page, V7X_SUBLANES, hd), index)

    page_specs = [page_spec(s) for s in range(n_pages)]
    return pl.pallas_call(
        functools.partial(_decode_attend_kernel, n_pages=n_pages, scale=hd ** -0.5),
        grid_spec=pltpu.PrefetchScalarGridSpec(
            num_scalar_prefetch=2,
            grid=(batch, heads),
            in_specs=[tok_spec, tok_spec, tok_spec] + page_specs + page_specs,
            out_specs=tok_spec,
        ),
        out_shape=jax.ShapeDtypeStruct((batch, heads, hd), F32),
        compiler_params=_params("parallel", "arbitrary"),
        name="moba_decode_attend",
    )(page_table_flat, top_flat, q, k_new, v_new,
      *([cache_k] * n_pages), *([cache_v] * n_pages))


def _trunk(x, weights, *, batch, seq, gla_s0=None, cache=None):
    w = weights
    n_gla = w["w_qkvr"].shape[0]
    depth = w["w_up"].shape[0]
    width = w["w_kv"].shape[2] // 2
    heads = MOBA_HEADS
    hd = width // heads
    h = x
    states = []
    k_new = v_new = kmean = None
    for l in range(depth):
        if l < n_gla:
            proj, = norm_matmul(h, w["norm_mix"][l], w["w_qkvr"], l, tn=1536)
            b = gla_gate(h, w["norm_mix"][l], w["w_z"], w["w_gate"], w["b_gate"], l,
                         chunk=min(GLA_CHUNK, seq))
            if cache is None:
                og, s = gla_prompt(proj, b, w["head_norm"], l, batch=batch, seq=seq)
            else:
                og, s = gla_step(proj, b, w["head_norm"], gla_s0, l)
            states.append(s)
            h = matmul_residual(og, w["w_out"], h, l)
        else:
            i = l - n_gla
            q, = norm_matmul(h, w["norm_mix"][l], w["w_q"], i, tn=2048)
            if cache is None:
                o = moba_prompt(q, k_new, v_new, batch=batch, seq=seq)
            else:
                cache_k, cache_v, page_table_flat, n_pages = cache
                q3 = q.reshape(batch, heads, hd)
                top = decode_select(q3, kmean)
                o = decode_attend(q3, k_new.reshape(batch, heads, hd),
                                  v_new.reshape(batch, heads, hd), cache_k, cache_v,
                                  page_table_flat, top.reshape(-1),
                                  n_pages_seq=n_pages).reshape(batch, width)
            h = matmul_residual(o, w["w_o"], h, i)
        h = mlp_residual(h, w["norm_mlp"], w["w_up"], w["w_down"], l)
        if l == n_gla - 1:
            k_new, v_new = norm_matmul(h, w["kv_norm"], w["w_kv"], 0, n_split=2)
            if cache is not None:
                kmean = paged_kmean(cache[0], cache[2], batch=batch, n_pages=cache[3])
    y = rmsnorm(h, w["norm_final"])
    return y, jnp.stack(states), k_new, v_new


def kernel(x_prompt, x_sample, state_gla, cache_k, cache_v, page_table, norm_mix, norm_mlp,
           w_mlp_up, w_mlp_down, w_in_a, w_gate_a, b_gate_a, head_norm_a, w_out_a, kv_norm,
           w_kv, w_q_b, w_o_b, norm_final):
    batch, seq, d = x_prompt.shape
    dec_batch, dec_seq, _ = x_sample.shape
    assert dec_seq == 1
    n_qkvr = w_in_a.shape[2] - GLA_GATE_RANK
    weights = dict(
        norm_mix=norm_mix, norm_mlp=norm_mlp, norm_final=norm_final, kv_norm=kv_norm,
        head_norm=head_norm_a, w_gate=w_gate_a, b_gate=b_gate_a,
        w_qkvr=w_in_a[:, :, :n_qkvr].astype(BF16),
        w_z=w_in_a[:, :, n_qkvr:],
        w_out=w_out_a.astype(BF16),
        w_up=w_mlp_up.astype(BF16),
        w_down=w_mlp_down.astype(BF16),
        w_kv=w_kv.astype(BF16)[None],
        w_q=w_q_b.astype(BF16),
        w_o=w_o_b.astype(BF16),
    )
    heads, hd = cache_k.shape[2], cache_k.shape[3]

    y_p, s_p, k_p, v_p = _trunk(x_prompt.reshape(batch * seq, d), weights, batch=batch, seq=seq)
    cache = (cache_k, cache_v, page_table.reshape(-1), page_table.shape[1])
    y_s, s_s, k_s, v_s = _trunk(x_sample.reshape(dec_batch, d), weights, batch=dec_batch, seq=1,
                                gla_s0=state_gla, cache=cache)
    return (y_p.reshape(batch, seq, d), y_s.reshape(dec_batch, 1, d), s_p, s_s,
            k_p.reshape(batch, seq, heads, hd), v_p.reshape(batch, seq, heads, hd),
            k_s.reshape(dec_batch, 1, heads, hd), v_s.reshape(dec_batch, 1, heads, hd))
```

```python
import functools

import jax
import jax.numpy as jnp
from jax import lax
from jax.experimental import pallas as pl
from jax.experimental.pallas import tpu as pltpu

F32 = jnp.float32
BF16 = jnp.bfloat16
HIGHEST = lax.Precision.HIGHEST

RMS_EPS = 1e-6
GLA_HEADS = 4
GLA_GATE_RANK = 16
GLA_GATE_TAU = 16.0
GLA_CHUNK = 64
GLA_SUB = 16
MOBA_HEADS = 16
MOBA_BLOCK = 256
MOBA_TOP_K = 3
LOG2_E = 1.4426950408889634

V7X_VMEM_BYTES = 64 * 1024 * 1024
VMEM_LIMIT_BYTES = V7X_VMEM_BYTES - 8 * 1024 * 1024


def _params(*semantics):
    return pltpu.CompilerParams(dimension_semantics=semantics,
                                vmem_limit_bytes=VMEM_LIMIT_BYTES)


def _rms_normalize(x, gain):
    ms = jnp.mean(x * x, axis=-1, keepdims=True)
    return x * lax.rsqrt(ms + RMS_EPS) * gain


def _row_tile(m, want):
    return want if m % want == 0 else m


def _col_tile(n, want):
    while n % want:
        want //= 2
    return want


def _contract_last(a, b, **kw):
    return lax.dot_general(a, b, (((1,), (1,)), ((), ())), preferred_element_type=F32, **kw)


def _contract_first(a, b):
    return lax.dot_general(a, b, (((0,), (0,)), ((), ())), preferred_element_type=F32)


def _norm_matmul_kernel(x_ref, gain_ref, *refs, n_split):
    w_refs, o_refs, xn_ref = refs[:n_split], refs[n_split:2 * n_split], refs[2 * n_split]

    @pl.when(pl.program_id(1) == 0)
    def _():
        xn_ref[...] = _rms_normalize(x_ref[...], gain_ref[...]).astype(xn_ref.dtype)

    for w_ref, o_ref in zip(w_refs, o_refs):
        o_ref[...] = jnp.dot(xn_ref[...], w_ref[...], preferred_element_type=F32)


def norm_matmul(x, gain, w, layer, *, n_cols=None, n_split=1, tm=1024, tn=1024):
    m, d = x.shape
    n = w.shape[2] if n_cols is None else n_cols
    n_out = n // n_split
    tm = _row_tile(m, tm)
    tn = _col_tile(n_out, tn)
    per_split = n_out // tn

    def w_spec(s):
        return pl.BlockSpec((None, d, tn), lambda i, j: (layer, 0, s * per_split + j))

    return pl.pallas_call(
        functools.partial(_norm_matmul_kernel, n_split=n_split),
        grid=(m // tm, per_split),
        in_specs=[pl.BlockSpec((tm, d), lambda i, j: (i, 0)),
                  pl.BlockSpec((1, d), lambda i, j: (0, 0))]
                 + [w_spec(s) for s in range(n_split)],
        out_specs=[pl.BlockSpec((tm, tn), lambda i, j: (i, j))] * n_split,
        out_shape=[jax.ShapeDtypeStruct((m, n_out), F32)] * n_split,
        scratch_shapes=[pltpu.VMEM((tm, d), BF16)],
        compiler_params=_params("parallel", "arbitrary"),
        name="norm_matmul",
    )(x, gain.reshape(1, d), *([w] * n_split))


def _split_bf16(a):
    hi = a.astype(BF16)
    return hi, a - hi.astype(F32)


def _dot_split(a, b):
    a_hi, a_rest = _split_bf16(a)
    b_hi, b_rest = _split_bf16(b)
    a_lo, b_lo = a_rest.astype(BF16), b_rest.astype(BF16)
    return (jnp.dot(a_hi, b_hi, preferred_element_type=F32)
            + jnp.dot(a_lo, b_hi, preferred_element_type=F32)
            + jnp.dot(a_hi, b_lo, preferred_element_type=F32))


def _gla_gate_kernel(x_ref, gain_ref, wz_ref, wg_ref, bg_ref, o_ref, *, chunk):
    xn = _rms_normalize(x_ref[...], gain_ref[...])
    z = _dot_split(xn, wz_ref[...])
    logit = _dot_split(z, wg_ref[...]) + bg_ref[...]
    log_sig = jnp.minimum(logit, 0.0) - jnp.log(1.0 + jnp.exp(-jnp.abs(logit)))
    g = log_sig / GLA_GATE_TAU
    if chunk > 1:
        rows = g.shape[0]
        row = lax.broadcasted_iota(jnp.int32, (rows, rows), 0)
        col = lax.broadcasted_iota(jnp.int32, (rows, rows), 1)
        same_chunk_past = (row >= col) & (row // chunk == col // chunk)
        ones = jnp.where(same_chunk_past, 1.0, 0.0).astype(BF16)
        g_hi, g_rest = _split_bf16(g)
        g_mid, g_rest = _split_bf16(g_rest)
        g = (jnp.dot(ones, g_hi, preferred_element_type=F32)
             + jnp.dot(ones, g_mid, preferred_element_type=F32)
             + jnp.dot(ones, g_rest.astype(BF16), preferred_element_type=F32))
    o_ref[...] = g


def gla_gate(x, gain, w_z, w_gate, b_gate, layer, *, chunk, tm=256):
    m, d = x.shape
    rank, n = w_gate.shape[1:]
    tm = _row_tile(m, tm)
    assert chunk == 1 or tm % chunk == 0
    return pl.pallas_call(
        functools.partial(_gla_gate_kernel, chunk=chunk),
        grid=(m // tm,),
        in_specs=[
            pl.BlockSpec((tm, d), lambda i: (i, 0)),
            pl.BlockSpec((1, d), lambda i: (0, 0)),
            pl.BlockSpec((None, d, rank), lambda i: (layer, 0, 0)),
            pl.BlockSpec((None, rank, n), lambda i: (layer, 0, 0)),
            pl.BlockSpec((None, 1, n), lambda i: (layer, 0, 0)),
        ],
        out_specs=pl.BlockSpec((tm, n), lambda i: (i, 0)),
        out_shape=jax.ShapeDtypeStruct((m, n), F32),
        compiler_params=_params("parallel"),
        name="gla_gate",
    )(x, gain.reshape(1, d), w_z, w_gate, b_gate.reshape(b_gate.shape[0], 1, n))


def _gla_chunk(q, k, v, b, st):
    c, dk = q.shape
    n_sub = c // GLA_SUB
    row = lax.broadcasted_iota(jnp.int32, (c, c), 0)
    col = lax.broadcasted_iota(jnp.int32, (c, c), 1)

    def rows_of(r, n):
        return jnp.concatenate(
            [jnp.broadcast_to(b[j * GLA_SUB + r:j * GLA_SUB + r + 1, :], (GLA_SUB, dk))
             for j in range(n)], axis=0)

    b_end = b[c - 1:c, :]
    v_b = v.astype(BF16)

    k_end = (k * jnp.exp(b_end - b)).astype(BF16)
    st_new = st * jnp.exp(b_end) + _contract_first(v_b, k_end)

    from_start = b - rows_of(0, n_sub)
    q_diag = q * jnp.exp(from_start)
    k_diag = k * jnp.exp(-from_start)
    att_diag = _contract_last(q_diag.astype(BF16), k_diag.astype(BF16))
    same_sub = (row // GLA_SUB) == (col // GLA_SUB)
    att = jnp.where(same_sub & (row >= col), att_diag, 0.0)

    if n_sub > 1:
        past = c - GLA_SUB
        k_off = (k[:past] * jnp.exp(rows_of(GLA_SUB - 1, n_sub - 1) - b[:past])).astype(BF16)
        q_parts, k_parts = [], []
        for j in range(n_sub - 1):
            lo = (j + 1) * GLA_SUB
            q_j = (q[lo:] * jnp.exp(b[lo:] - b[lo - 1:lo, :])).astype(BF16)
            q_parts.append(jnp.concatenate([jnp.zeros((lo, dk), BF16), q_j], axis=0))
            pieces = [k_off[lo - GLA_SUB:lo], jnp.zeros((c - lo, dk), BF16)]
            if j:
                pieces.insert(0, jnp.zeros((lo - GLA_SUB, dk), BF16))
            k_parts.append(jnp.concatenate(pieces, axis=0))
        att = att + _contract_last(jnp.concatenate(q_parts, axis=1),
                                   jnp.concatenate(k_parts, axis=1))

    o = _contract_last((q * jnp.exp(b)).astype(BF16), st.astype(BF16))
    o = o + jnp.dot(att.astype(BF16), v_b, preferred_element_type=F32)
    return o, st_new


def _gla_prompt_kernel(q_ref, k_ref, v_ref, r_ref, b_ref, hn_ref, og_ref, s_ref, st_ref,
                       *, n_chunks, q_scale, unroll):
    t = pl.program_id(2)

    @pl.when(t == 0)
    def _():
        st_ref[...] = jnp.zeros_like(st_ref)

    def chunk(ci, carry):
        rows = pl.ds(pl.multiple_of(ci * GLA_CHUNK, GLA_CHUNK), GLA_CHUNK)
        o, st_new = _gla_chunk(q_ref[rows, :] * q_scale, k_ref[rows, :], v_ref[rows, :],
                               b_ref[rows, :], st_ref[...])
        st_ref[...] = st_new
        r = r_ref[rows, :]
        silu = r / (1.0 + jnp.exp(-r))
        og_ref[rows, :] = (_rms_normalize(o, hn_ref[...]) * silu).astype(og_ref.dtype)
        return carry

    lax.fori_loop(0, n_chunks, chunk, 0, unroll=unroll)

    @pl.when(t == pl.num_programs(2) - 1)
    def _():
        s_ref[...] = st_ref[...].T


def gla_prompt(proj, b, head_norm, layer, *, batch, seq, tt=512, unroll=4):
    m = batch * seq
    qk = b.shape[1]
    dk = qk // GLA_HEADS
    vw = (proj.shape[1] - 2 * qk) // 2
    dv = vw // GLA_HEADS
    tt = min(tt, seq)
    assert seq % tt == 0 and tt % GLA_CHUNK == 0 and dv % dk == 0
    nt = seq // tt
    k_col0 = qk // dk
    v_col0 = 2 * qk // dv
    r_col0 = (2 * qk + vw) // dv
    kern = functools.partial(_gla_prompt_kernel, n_chunks=tt // GLA_CHUNK, q_scale=dk ** -0.5,
                             unroll=unroll)
    return pl.pallas_call(
        kern,
        grid=(batch, GLA_HEADS, nt),
        in_specs=[
            pl.BlockSpec((tt, dk), lambda bi, h, t: (bi * nt + t, h)),
            pl.BlockSpec((tt, dk), lambda bi, h, t: (bi * nt + t, k_col0 + h)),
            pl.BlockSpec((tt, dv), lambda bi, h, t: (bi * nt + t, v_col0 + h)),
            pl.BlockSpec((tt, dv), lambda bi, h, t: (bi * nt + t, r_col0 + h)),
            pl.BlockSpec((tt, dk), lambda bi, h, t: (bi * nt + t, h)),
            pl.BlockSpec((None, 1, dv), lambda bi, h, t: (layer, 0, 0)),
        ],
        out_specs=[
            pl.BlockSpec((tt, dv), lambda bi, h, t: (bi * nt + t, h)),
            pl.BlockSpec((None, None, dk, dv), lambda bi, h, t: (bi, h, 0, 0)),
        ],
        out_shape=[
            jax.ShapeDtypeStruct((m, vw), BF16),
            jax.ShapeDtypeStruct((batch, GLA_HEADS, dk, dv), F32),
        ],
        scratch_shapes=[pltpu.VMEM((dv, dk), F32)],
        compiler_params=_params("parallel", "parallel", "arbitrary"),
        name="gla_prompt",
    )(proj, proj, proj, proj, b, head_norm.reshape(head_norm.shape[0], 1, dv))


def _gla_step_kernel(q_ref, k_ref, g_ref, v_ref, r_ref, hn_ref, s0_ref, og_ref, s_ref,
                     *, q_scale):
    s_new = jnp.exp(g_ref[...]) * s0_ref[...] + k_ref[...] * v_ref[...]
    s_ref[...] = s_new
    o = jnp.sum((q_ref[...] * q_scale) * s_new, axis=0, keepdims=True)
    r = r_ref[...]
    silu = r / (1.0 + jnp.exp(-r))
    og_ref[...] = _rms_normalize(o, hn_ref[...]) * silu


def gla_step(proj, g, head_norm, s0, layer):
    batch = proj.shape[0]
    qk = g.shape[1]
    dk = qk // GLA_HEADS
    vw = (proj.shape[1] - 2 * qk) // 2
    dv = vw // GLA_HEADS
    col = lambda a: a.reshape(batch, GLA_HEADS, dk, 1)
    row = lambda a: a.reshape(batch, GLA_HEADS, 1, dv)
    q_c = col(proj[:, :qk])
    k_c = col(proj[:, qk:2 * qk])
    g_c = col(g)
    v_r = row(proj[:, 2 * qk:2 * qk + vw])
    r_r = row(proj[:, 2 * qk + vw:])
    col_spec = pl.BlockSpec((None, None, dk, 1), lambda bi, h: (bi, h, 0, 0))
    row_spec = pl.BlockSpec((None, None, 1, dv), lambda bi, h: (bi, h, 0, 0))
    st_spec = pl.BlockSpec((None, None, dk, dv), lambda bi, h: (bi, h, 0, 0))
    og, s = pl.pallas_call(
        functools.partial(_gla_step_kernel, q_scale=dk ** -0.5),
        grid=(batch, GLA_HEADS),
        in_specs=[col_spec, col_spec, col_spec, row_spec, row_spec,
                  pl.BlockSpec((None, 1, dv), lambda bi, h: (layer, 0, 0)),
                  pl.BlockSpec((None, None, None, dk, dv), lambda bi, h: (layer, bi, h, 0, 0))],
        out_specs=[row_spec, st_spec],
        out_shape=[jax.ShapeDtypeStruct((batch, GLA_HEADS, 1, dv), F32),
                   jax.ShapeDtypeStruct(s0.shape[1:], F32)],
        compiler_params=_params("parallel", "parallel"),
        name="gla_step",
    )(q_c, k_c, g_c, v_r, r_r, head_norm.reshape(head_norm.shape[0], 1, dv), s0)
    return og.reshape(batch, vw), s


def _matmul_residual_kernel(a_ref, w_ref, h_ref, o_ref):
    o_ref[...] = h_ref[...] + jnp.dot(a_ref[...].astype(BF16), w_ref[...],
                                      preferred_element_type=F32)


def matmul_residual(a, w, h, layer, *, tm=512, tn=2048):
    m, kd = a.shape
    n = w.shape[2]
    tm = _row_tile(m, tm)
    tn = _col_tile(n, tn)
    return pl.pallas_call(
        _matmul_residual_kernel,
        grid=(m // tm, n // tn),
        in_specs=[
            pl.BlockSpec((tm, kd), lambda i, j: (i, 0)),
            pl.BlockSpec((None, kd, tn), lambda i, j: (layer, 0, j)),
            pl.BlockSpec((tm, tn), lambda i, j: (i, j)),
        ],
        out_specs=pl.BlockSpec((tm, tn), lambda i, j: (i, j)),
        out_shape=jax.ShapeDtypeStruct((m, n), F32),
        compiler_params=_params("parallel", "parallel"),
        name="matmul_residual",
    )(a, w, h)


def _mlp_kernel(h_ref, gain_ref, wu_ref, wd_ref, o_ref, xn_ref):
    @pl.when(pl.program_id(1) == 0)
    def _():
        h = h_ref[...]
        xn_ref[...] = _rms_normalize(h, gain_ref[...]).astype(xn_ref.dtype)
        o_ref[...] = h

    u = jnp.maximum(jnp.dot(xn_ref[...], wu_ref[...], preferred_element_type=F32), 0.0)
    o_ref[...] += jnp.dot((u * u).astype(BF16), wd_ref[...], preferred_element_type=F32)


def mlp_residual(h, gain, w_up, w_down, layer, *, tm=1024, tf=512):
    m, d = h.shape
    ff = w_up.shape[2]
    tm = _row_tile(m, tm)
    tf = _col_tile(ff, tf)
    return pl.pallas_call(
        _mlp_kernel,
        grid=(m // tm, ff // tf),
        in_specs=[
            pl.BlockSpec((tm, d), lambda i, f: (i, 0)),
            pl.BlockSpec((None, 1, d), lambda i, f: (layer, 0, 0)),
            pl.BlockSpec((None, d, tf), lambda i, f: (layer, 0, f)),
            pl.BlockSpec((None, tf, d), lambda i, f: (layer, f, 0)),
        ],
        out_specs=pl.BlockSpec((tm, d), lambda i, f: (i, 0)),
        out_shape=jax.ShapeDtypeStruct((m, d), F32),
        scratch_shapes=[pltpu.VMEM((tm, d), BF16)],
        compiler_params=_params("parallel", "arbitrary"),
        name="mlp_residual",
    )(h, gain.reshape(gain.shape[0], 1, d), w_up, w_down)


def _rmsnorm_kernel(x_ref, gain_ref, o_ref):
    o_ref[...] = _rms_normalize(x_ref[...], gain_ref[...])


def rmsnorm(x, gain, *, tm=512):
    m, d = x.shape
    tm = _row_tile(m, tm)
    return pl.pallas_call(
        _rmsnorm_kernel,
        grid=(m // tm,),
        in_specs=[pl.BlockSpec((tm, d), lambda i: (i, 0)),
                  pl.BlockSpec((1, d), lambda i: (0, 0))],
        out_specs=pl.BlockSpec((tm, d), lambda i: (i, 0)),
        out_shape=jax.ShapeDtypeStruct((m, d), F32),
        compiler_params=_params("parallel"),
        name="rmsnorm",
    )(x, gain.reshape(1, d))


def _top_blocks(gate, is_past, axis):
    n_blocks = gate.shape[axis]
    idx = lax.broadcasted_iota(jnp.int32, gate.shape, axis)
    neg_inf = jnp.float32(-jnp.inf)
    left = jnp.where(is_past, gate, neg_inf)
    chosen = jnp.zeros(gate.shape, dtype=jnp.bool_)
    picks = []
    for _ in range(MOBA_TOP_K):
        best = jnp.max(left, axis=axis, keepdims=True)
        is_best = (left == best) & (best > neg_inf)
        first = jnp.min(jnp.where(is_best, idx, n_blocks), axis=axis, keepdims=True)
        pick = idx == first
        chosen = chosen | pick
        left = jnp.where(pick, neg_inf, left)
        picks.append(first)
    return chosen, picks


def _moba_prompt_kernel(q_ref, k_ref, v_ref, o_ref, kb_ref, vt_ref, kmean_ref,
                        *, n_blocks, n_heads, hd, scale, tq):
    ti = pl.program_id(2)
    blk = MOBA_BLOCK
    heads = range(n_heads)

    @pl.when(ti == 0)
    def _():
        for j in heads:
            cols = slice(j * hd, (j + 1) * hd)
            for n in range(n_blocks):
                k_blk = k_ref[n * blk:(n + 1) * blk, cols]
                kb_ref[j, n] = k_blk.astype(BF16)
                vt_ref[j, n] = v_ref[n * blk:(n + 1) * blk, cols].T.astype(BF16)
                kmean_ref[j, n:n + 1, :] = jnp.mean(k_blk, axis=0, keepdims=True)

    q_pos = ti * tq + lax.broadcasted_iota(jnp.int32, (1, tq), 1)
    q_block = q_pos // blk
    block_id = lax.broadcasted_iota(jnp.int32, (n_blocks, tq), 0)
    key_off = lax.broadcasted_iota(jnp.int32, (blk, tq), 0)
    neg_inf = jnp.float32(-jnp.inf)

    q_bs, visible, carry0 = [], [], []
    for j in heads:
        q = q_ref[:, j * hd:(j + 1) * hd]
        gate = _contract_last(kmean_ref[j], q, precision=HIGHEST)
        chosen, _ = _top_blocks(gate, block_id < q_block, axis=0)
        visible.append(jnp.where(chosen | (block_id == q_block), 1.0, 0.0))
        q_bs.append((q * (scale * LOG2_E)).astype(BF16))
        carry0.append((jnp.full((1, tq), neg_inf), jnp.zeros((1, tq), F32),
                       jnp.zeros((hd, tq), F32)))

    def key_block(n, carry):
        q_off = q_pos - n * blk
        raw = [_contract_last(kb_ref[j, n], q_bs[j]) for j in heads]
        partial = []
        for j in heads:
            m_run, l_run, _ = carry[j]
            vis = jnp.max(jnp.where(block_id == n, visible[j], 0.0), axis=0, keepdims=True)
            threshold = jnp.where(vis > 0.0, q_off, -1)
            s = jnp.where(key_off <= threshold, raw[j], neg_inf)
            m_new = jnp.maximum(m_run, jnp.max(s, axis=0, keepdims=True))
            seen = m_new > neg_inf
            m_safe = jnp.where(seen, m_new, 0.0)
            alpha = jnp.where(seen, jnp.exp2(m_run - m_safe), 0.0)
            p = jnp.exp2(s - m_safe)
            l_new = alpha * l_run + jnp.sum(p, axis=0, keepdims=True)
            partial.append((m_new, l_new, alpha, p.astype(BF16)))
        out = []
        for j in heads:
            m_new, l_new, alpha, p_b = partial[j]
            acc = alpha * carry[j][2] + jnp.dot(vt_ref[j, n], p_b,
                                                preferred_element_type=F32)
            out.append((m_new, l_new, acc))
        return tuple(out)

    final = lax.fori_loop(0, (ti + 1) * (tq // blk), key_block, tuple(carry0))
    for j in heads:
        _, l_fin, acc = final[j]
        o_ref[:, j * hd:(j + 1) * hd] = (acc / l_fin).T.astype(o_ref.dtype)


def moba_prompt(q, k, v, *, batch, seq, heads_per_step=4, tq=512):
    m, width = q.shape
    hd = width // MOBA_HEADS
    tq = min(tq, seq)
    assert seq % tq == 0 and tq % MOBA_BLOCK == 0 and MOBA_HEADS % heads_per_step == 0
    n_blocks = seq // MOBA_BLOCK
    n_tiles = seq // tq
    gw = heads_per_step * hd
    kern = functools.partial(_moba_prompt_kernel, n_blocks=n_blocks, n_heads=heads_per_step,
                             hd=hd, scale=hd ** -0.5, tq=tq)
    return pl.pallas_call(
        kern,
        grid=(batch, MOBA_HEADS // heads_per_step, n_tiles),
        in_specs=[
            pl.BlockSpec((tq, gw), lambda b, h, i: (b * n_tiles + i, h)),
            pl.BlockSpec((seq, gw), lambda b, h, i: (b, h)),
            pl.BlockSpec((seq, gw), lambda b, h, i: (b, h)),
        ],
        out_specs=pl.BlockSpec((tq, gw), lambda b, h, i: (b * n_tiles + i, h)),
        out_shape=jax.ShapeDtypeStruct((m, width), BF16),
        scratch_shapes=[pltpu.VMEM((heads_per_step, n_blocks, MOBA_BLOCK, hd), BF16),
                        pltpu.VMEM((heads_per_step, n_blocks, hd, MOBA_BLOCK), BF16),
                        pltpu.VMEM((heads_per_step, n_blocks, hd), F32)],
        compiler_params=_params("parallel", "parallel", "arbitrary"),
        name="moba_prompt",
    )(q, k, v)


def _paged_kmean_kernel(pt_ref, *refs, blocks_per_step, pages_per_block, block_len):
    n_in = blocks_per_step * pages_per_block
    page_refs, o_ref = refs[:n_in], refs[n_in]
    step = pl.program_id(1)
    for i in range(blocks_per_step):
        total = jnp.sum(page_refs[i * pages_per_block][...], axis=0)
        for page_ref in page_refs[i * pages_per_block + 1:(i + 1) * pages_per_block]:
            total = total + jnp.sum(page_ref[...], axis=0)
        o_ref[step * blocks_per_step + i] = total / block_len


def paged_kmean(cache_k, page_table_flat, *, batch, n_pages, blocks_per_step=2):
    _, page, heads, hd = cache_k.shape
    ppb = MOBA_BLOCK // page
    assert MOBA_BLOCK % page == 0 and n_pages % (ppb * blocks_per_step) == 0
    n_past = n_pages // ppb
    n_in = ppb * blocks_per_step

    def page_spec(slot):
        return pl.BlockSpec((None, page, heads, hd),
                            lambda b, n, pt: (pt[b * n_pages + n * n_in + slot], 0, 0, 0))

    return pl.pallas_call(
        functools.partial(_paged_kmean_kernel, blocks_per_step=blocks_per_step,
                          pages_per_block=ppb, block_len=MOBA_BLOCK),
        grid_spec=pltpu.PrefetchScalarGridSpec(
            num_scalar_prefetch=1,
            grid=(batch, n_past // blocks_per_step),
            in_specs=[page_spec(s) for s in range(n_in)],
            out_specs=pl.BlockSpec((None, n_past, heads, hd), lambda b, n, pt: (b, 0, 0, 0)),
        ),
        out_shape=jax.ShapeDtypeStruct((batch, n_past, heads, hd), F32),
        compiler_params=_params("parallel", "arbitrary"),
        name="paged_kmean",
    )(page_table_flat, *([cache_k] * n_in))


def _decode_select_kernel(q_ref, kmean_ref, o_ref):
    gate = jnp.sum(kmean_ref[...] * q_ref[...], axis=-1, keepdims=True)
    _, picks = _top_blocks(gate, jnp.full(gate.shape, True), axis=0)
    for r, first in enumerate(picks):
        o_ref[r] = first[0]


def decode_select(q, kmean):
    batch, n_past, heads, hd = kmean.shape
    assert n_past >= MOBA_TOP_K
    return pl.pallas_call(
        _decode_select_kernel,
        grid=(batch,),
        in_specs=[pl.BlockSpec((None, heads, hd), lambda b: (b, 0, 0)),
                  pl.BlockSpec((None, n_past, heads, hd), lambda b: (b, 0, 0, 0))],
        out_specs=pl.BlockSpec((None, MOBA_TOP_K, heads, 1), lambda b: (b, 0, 0, 0)),
        out_shape=jax.ShapeDtypeStruct((batch, MOBA_TOP_K, heads, 1), jnp.int32),
        compiler_params=_params("parallel"),
        name="moba_decode_select",
    )(q, kmean)


def _decode_attend_kernel(pt_ref, top_ref, q_ref, kn_ref, vn_ref, ck_ref, cv_ref, o_ref,
                          kbuf_ref, vbuf_ref, sem_ref, *, heads, pages_per_block, n_pages_seq,
                          scale):
    b = pl.program_id(0)
    n_slots = MOBA_TOP_K * pages_per_block

    def head_copies(h):
        copies = []
        for r in range(MOBA_TOP_K):
            block = top_ref[(b * MOBA_TOP_K + r) * heads + h]
            for pg in range(pages_per_block):
                page = pt_ref[b * n_pages_seq + block * pages_per_block + pg]
                slot = r * pages_per_block + pg
                copies.append(pltpu.make_async_copy(
                    ck_ref.at[page, :, h, :], kbuf_ref.at[h, slot], sem_ref.at[0, h]))
                copies.append(pltpu.make_async_copy(
                    cv_ref.at[page, :, h, :], vbuf_ref.at[h, slot], sem_ref.at[1, h]))
        return copies

    all_copies = [head_copies(h) for h in range(heads)]
    for copies in all_copies:
        for cp in copies:
            cp.start()

    for h in range(heads):
        for cp in all_copies[h]:
            cp.wait()
        q = q_ref[h:h + 1, :]
        k_new, v_new = kn_ref[h:h + 1, :], vn_ref[h:h + 1, :]
        s_own = jnp.sum(q * k_new, axis=-1, keepdims=True) * scale
        s_past = [jnp.sum(kbuf_ref[h, i] * q, axis=-1, keepdims=True) * scale
                  for i in range(n_slots)]
        m = s_own
        for s in s_past:
            m = jnp.maximum(m, jnp.max(s, axis=0, keepdims=True))
        p_own = jnp.exp(s_own - m)
        denom = p_own
        acc = p_own * v_new
        for i, s in enumerate(s_past):
            p = jnp.exp(s - m)
            denom = denom + jnp.sum(p, axis=0, keepdims=True)
            acc = acc + jnp.sum(p * vbuf_ref[h, i], axis=0, keepdims=True)
        o_ref[h:h + 1, :] = acc / denom


def decode_attend(q, k_new, v_new, cache_k, cache_v, page_table_flat, top_flat, *, n_pages_seq):
    batch, heads, hd = q.shape
    page = cache_k.shape[1]
    ppb = MOBA_BLOCK // page
    n_slots = MOBA_TOP_K * ppb
    tok_spec = pl.BlockSpec((None, heads, hd), lambda b, pt, tp: (b, 0, 0))
    hbm_spec = pl.BlockSpec(memory_space=pl.ANY)
    return pl.pallas_call(
        functools.partial(_decode_attend_kernel, heads=heads, pages_per_block=ppb,
                          n_pages_seq=n_pages_seq, scale=hd ** -0.5),
        grid_spec=pltpu.PrefetchScalarGridSpec(
            num_scalar_prefetch=2,
            grid=(batch,),
            in_specs=[tok_spec, tok_spec, tok_spec, hbm_spec, hbm_spec],
            out_specs=tok_spec,
            scratch_shapes=[pltpu.VMEM((heads, n_slots, page, hd), F32),
                            pltpu.VMEM((heads, n_slots, page, hd), F32),
                            pltpu.SemaphoreType.DMA((2, heads))],
        ),
        out_shape=jax.ShapeDtypeStruct((batch, heads, hd), F32),
        compiler_params=_params("arbitrary"),
        name="moba_decode_attend",
    )(page_table_flat, top_flat, q, k_new, v_new, cache_k, cache_v)


def _trunk(x, weights, *, batch, seq, gla_s0=None, cache=None):
    w = weights
    n_gla = w["w_in"].shape[0]
    depth = w["w_up"].shape[0]
    width = w["w_kv"].shape[2] // 2
    heads = MOBA_HEADS
    hd = width // heads
    h = x
    states = []
    k_new = v_new = kmean = None
    for l in range(depth):
        if l < n_gla:
            proj, = norm_matmul(h, w["norm_mix"][l], w["w_in"], l, n_cols=w["n_qkvr"], tn=1536)
            b = gla_gate(h, w["norm_mix"][l], w["w_z"], w["w_gate"], w["b_gate"], l,
                         chunk=min(GLA_CHUNK, seq))
            if cache is None:
                og, s = gla_prompt(proj, b, w["head_norm"], l, batch=batch, seq=seq)
            else:
                og, s = gla_step(proj, b, w["head_norm"], gla_s0, l)
            states.append(s)
            h = matmul_residual(og, w["w_out"], h, l)
        else:
            i = l - n_gla
            q, = norm_matmul(h, w["norm_mix"][l], w["w_q"], i)
            if cache is None:
                o = moba_prompt(q, k_new, v_new, batch=batch, seq=seq)
            else:
                cache_k, cache_v, page_table_flat, n_pages = cache
                q3 = q.reshape(batch, heads, hd)
                top = decode_select(q3, kmean)
                o = decode_attend(q3, k_new.reshape(batch, heads, hd),
                                  v_new.reshape(batch, heads, hd), cache_k, cache_v,
                                  page_table_flat, top.reshape(-1),
                                  n_pages_seq=n_pages).reshape(batch, width)
            h = matmul_residual(o, w["w_o"], h, i)
        h = mlp_residual(h, w["norm_mlp"], w["w_up"], w["w_down"], l)
        if l == n_gla - 1:
            k_new, v_new = norm_matmul(h, w["kv_norm"], w["w_kv"], 0, n_split=2, tn=512)
            if cache is not None:
                kmean = paged_kmean(cache[0], cache[2], batch=batch, n_pages=cache[3])
    y = rmsnorm(h, w["norm_final"])
    return y, jnp.stack(states), k_new, v_new


def kernel(x_prompt, x_sample, state_gla, cache_k, cache_v, page_table, norm_mix, norm_mlp,
           w_mlp_up, w_mlp_down, w_in_a, w_gate_a, b_gate_a, head_norm_a, w_out_a, kv_norm,
           w_kv, w_q_b, w_o_b, norm_final):
    batch, seq, d = x_prompt.shape
    dec_batch, dec_seq, _ = x_sample.shape
    assert dec_seq == 1
    n_qkvr = w_in_a.shape[2] - GLA_GATE_RANK
    weights = dict(
        norm_mix=norm_mix, norm_mlp=norm_mlp, norm_final=norm_final, kv_norm=kv_norm,
        head_norm=head_norm_a, w_gate=w_gate_a, b_gate=b_gate_a,
        w_in=w_in_a.astype(BF16), n_qkvr=n_qkvr,
        w_z=w_in_a[:, :, n_qkvr:],
        w_out=w_out_a.astype(BF16),
        w_up=w_mlp_up.astype(BF16),
        w_down=w_mlp_down.astype(BF16),
        w_kv=w_kv.astype(BF16)[None],
        w_q=w_q_b.astype(BF16),
        w_o=w_o_b.astype(BF16),
    )
    heads, hd = cache_k.shape[2], cache_k.shape[3]

    y_p, s_p, k_p, v_p = _trunk(x_prompt.reshape(batch * seq, d), weights, batch=batch, seq=seq)
    cache = (cache_k, cache_v, page_table.reshape(-1), page_table.shape[1])
    y_s, s_s, k_s, v_s = _trunk(x_sample.reshape(dec_batch, d), weights, batch=dec_batch, seq=1,
                                gla_s0=state_gla, cache=cache)
    return (y_p.reshape(batch, seq, d), y_s.reshape(dec_batch, 1, d), s_p, s_s,
            k_p.reshape(batch, seq, heads, hd), v_p.reshape(batch, seq, heads, hd),
            k_s.reshape(dec_batch, 1, heads, hd), v_s.reshape(dec_batch, 1, heads, hd))
```

```python
import functools

import jax
import jax.numpy as jnp
from jax import lax
from jax.experimental import pallas as pl
from jax.experimental.pallas import tpu as pltpu

F32 = jnp.float32
BF16 = jnp.bfloat16
HIGHEST = lax.Precision.HIGHEST

RMS_EPS = 1e-6
GLA_HEADS = 4
GLA_GATE_RANK = 16
GLA_GATE_TAU = 16.0
GLA_CHUNK = 64
GLA_SUB = 16
MOBA_HEADS = 16
MOBA_BLOCK = 256
MOBA_TOP_K = 3
LOG2_E = 1.4426950408889634

V7X_VMEM_BYTES = 64 * 1024 * 1024
VMEM_LIMIT_BYTES = V7X_VMEM_BYTES - 8 * 1024 * 1024


def _params(*semantics):
    return pltpu.CompilerParams(dimension_semantics=semantics,
                                vmem_limit_bytes=VMEM_LIMIT_BYTES)


def _rms_normalize(x, gain):
    ms = jnp.mean(x * x, axis=-1, keepdims=True)
    return x * lax.rsqrt(ms + RMS_EPS) * gain


def _row_tile(m, want):
    return want if m % want == 0 else m


def _col_tile(n, want):
    while n % want:
        want //= 2
    return want


def _contract_last(a, b, **kw):
    return lax.dot_general(a, b, (((1,), (1,)), ((), ())), preferred_element_type=F32, **kw)


def _contract_first(a, b):
    return lax.dot_general(a, b, (((0,), (0,)), ((), ())), preferred_element_type=F32)


def _norm_matmul_kernel(x_ref, gain_ref, *refs, n_split):
    w_refs, o_refs, xn_ref = refs[:n_split], refs[n_split:2 * n_split], refs[2 * n_split]

    @pl.when(pl.program_id(1) == 0)
    def _():
        xn_ref[...] = _rms_normalize(x_ref[...], gain_ref[...]).astype(xn_ref.dtype)

    for w_ref, o_ref in zip(w_refs, o_refs):
        o_ref[...] = jnp.dot(xn_ref[...], w_ref[...], preferred_element_type=F32)


def norm_matmul(x, gain, w, layer, *, n_cols=None, n_split=1, tm=1024, tn=1024):
    m, d = x.shape
    n = w.shape[2] if n_cols is None else n_cols
    n_out = n // n_split
    tm = _row_tile(m, tm)
    tn = _col_tile(n_out, tn)
    per_split = n_out // tn

    def w_spec(s):
        return pl.BlockSpec((None, d, tn), lambda i, j: (layer, 0, s * per_split + j))

    return pl.pallas_call(
        functools.partial(_norm_matmul_kernel, n_split=n_split),
        grid=(m // tm, per_split),
        in_specs=[pl.BlockSpec((tm, d), lambda i, j: (i, 0)),
                  pl.BlockSpec((1, d), lambda i, j: (0, 0))]
                 + [w_spec(s) for s in range(n_split)],
        out_specs=[pl.BlockSpec((tm, tn), lambda i, j: (i, j))] * n_split,
        out_shape=[jax.ShapeDtypeStruct((m, n_out), F32)] * n_split,
        scratch_shapes=[pltpu.VMEM((tm, d), BF16)],
        compiler_params=_params("parallel", "arbitrary"),
        name="norm_matmul",
    )(x, gain.reshape(1, d), *([w] * n_split))


def _split_bf16(a):
    hi = a.astype(BF16)
    return hi, a - hi.astype(F32)


def _dot_split(a, b):
    a_hi, a_rest = _split_bf16(a)
    b_hi, b_rest = _split_bf16(b)
    a_lo, b_lo = a_rest.astype(BF16), b_rest.astype(BF16)
    return (jnp.dot(a_hi, b_hi, preferred_element_type=F32)
            + jnp.dot(a_lo, b_hi, preferred_element_type=F32)
            + jnp.dot(a_hi, b_lo, preferred_element_type=F32))


def _gla_gate_kernel(x_ref, gain_ref, wz_ref, wg_ref, bg_ref, o_ref, *, chunk):
    xn = _rms_normalize(x_ref[...], gain_ref[...])
    z = _dot_split(xn, wz_ref[...])
    logit = _dot_split(z, wg_ref[...]) + bg_ref[...]
    log_sig = jnp.minimum(logit, 0.0) - jnp.log(1.0 + jnp.exp(-jnp.abs(logit)))
    g = log_sig / GLA_GATE_TAU
    if chunk > 1:
        rows = g.shape[0]
        row = lax.broadcasted_iota(jnp.int32, (rows, rows), 0)
        col = lax.broadcasted_iota(jnp.int32, (rows, rows), 1)
        same_chunk_past = (row >= col) & (row // chunk == col // chunk)
        ones = jnp.where(same_chunk_past, 1.0, 0.0).astype(BF16)
        g_hi, g_rest = _split_bf16(g)
        g_mid, g_rest = _split_bf16(g_rest)
        g = (jnp.dot(ones, g_hi, preferred_element_type=F32)
             + jnp.dot(ones, g_mid, preferred_element_type=F32)
             + jnp.dot(ones, g_rest.astype(BF16), preferred_element_type=F32))
    o_ref[...] = g


def gla_gate(x, gain, w_z, w_gate, b_gate, layer, *, chunk, tm=256):
    m, d = x.shape
    rank, n = w_gate.shape[1:]
    tm = _row_tile(m, tm)
    assert chunk == 1 or tm % chunk == 0
    return pl.pallas_call(
        functools.partial(_gla_gate_kernel, chunk=chunk),
        grid=(m // tm,),
        in_specs=[
            pl.BlockSpec((tm, d), lambda i: (i, 0)),
            pl.BlockSpec((1, d), lambda i: (0, 0)),
            pl.BlockSpec((None, d, rank), lambda i: (layer, 0, 0)),
            pl.BlockSpec((None, rank, n), lambda i: (layer, 0, 0)),
            pl.BlockSpec((None, 1, n), lambda i: (layer, 0, 0)),
        ],
        out_specs=pl.BlockSpec((tm, n), lambda i: (i, 0)),
        out_shape=jax.ShapeDtypeStruct((m, n), F32),
        compiler_params=_params("parallel"),
        name="gla_gate",
    )(x, gain.reshape(1, d), w_z, w_gate, b_gate.reshape(b_gate.shape[0], 1, n))


def _gla_chunk(q, k, v, b, st):
    c, dk = q.shape
    n_sub = c // GLA_SUB
    row = lax.broadcasted_iota(jnp.int32, (c, c), 0)
    col = lax.broadcasted_iota(jnp.int32, (c, c), 1)

    def rows_of(r, n):
        return jnp.concatenate(
            [jnp.broadcast_to(b[j * GLA_SUB + r:j * GLA_SUB + r + 1, :], (GLA_SUB, dk))
             for j in range(n)], axis=0)

    b_end = b[c - 1:c, :]
    v_b = v.astype(BF16)

    k_end = (k * jnp.exp(b_end - b)).astype(BF16)
    st_new = st * jnp.exp(b_end) + _contract_first(v_b, k_end)

    from_start = b - rows_of(0, n_sub)
    q_diag = q * jnp.exp(from_start)
    k_diag = k * jnp.exp(-from_start)
    att_diag = _contract_last(q_diag.astype(BF16), k_diag.astype(BF16))
    same_sub = (row // GLA_SUB) == (col // GLA_SUB)
    att = jnp.where(same_sub & (row >= col), att_diag, 0.0)

    if n_sub > 1:
        past = c - GLA_SUB
        k_off = (k[:past] * jnp.exp(rows_of(GLA_SUB - 1, n_sub - 1) - b[:past])).astype(BF16)
        q_parts, k_parts = [], []
        for j in range(n_sub - 1):
            lo = (j + 1) * GLA_SUB
            q_j = (q[lo:] * jnp.exp(b[lo:] - b[lo - 1:lo, :])).astype(BF16)
            q_parts.append(jnp.concatenate([jnp.zeros((lo, dk), BF16), q_j], axis=0))
            pieces = [k_off[lo - GLA_SUB:lo], jnp.zeros((c - lo, dk), BF16)]
            if j:
                pieces.insert(0, jnp.zeros((lo - GLA_SUB, dk), BF16))
            k_parts.append(jnp.concatenate(pieces, axis=0))
        att = att + _contract_last(jnp.concatenate(q_parts, axis=1),
                                   jnp.concatenate(k_parts, axis=1))

    o = _contract_last((q * jnp.exp(b)).astype(BF16), st.astype(BF16))
    o = o + jnp.dot(att.astype(BF16), v_b, preferred_element_type=F32)
    return o, st_new


def _gla_prompt_kernel(q_ref, k_ref, v_ref, r_ref, b_ref, hn_ref, og_ref, s_ref, st_ref,
                       *, n_chunks, q_scale, unroll):
    t = pl.program_id(2)

    @pl.when(t == 0)
    def _():
        st_ref[...] = jnp.zeros_like(st_ref)

    def chunk(ci, carry):
        rows = pl.ds(pl.multiple_of(ci * GLA_CHUNK, GLA_CHUNK), GLA_CHUNK)
        o, st_new = _gla_chunk(q_ref[rows, :] * q_scale, k_ref[rows, :], v_ref[rows, :],
                               b_ref[rows, :], st_ref[...])
        st_ref[...] = st_new
        r = r_ref[rows, :]
        silu = r / (1.0 + jnp.exp(-r))
        og_ref[rows, :] = (_rms_normalize(o, hn_ref[...]) * silu).astype(og_ref.dtype)
        return carry

    lax.fori_loop(0, n_chunks, chunk, 0, unroll=unroll)

    @pl.when(t == pl.num_programs(2) - 1)
    def _():
        s_ref[...] = st_ref[...].T


def gla_prompt(proj, b, head_norm, layer, *, batch, seq, tt=512, unroll=4):
    m = batch * seq
    qk = b.shape[1]
    dk = qk // GLA_HEADS
    vw = (proj.shape[1] - 2 * qk) // 2
    dv = vw // GLA_HEADS
    tt = min(tt, seq)
    assert seq % tt == 0 and tt % GLA_CHUNK == 0 and dv % dk == 0
    nt = seq // tt
    k_col0 = qk // dk
    v_col0 = 2 * qk // dv
    r_col0 = (2 * qk + vw) // dv
    kern = functools.partial(_gla_prompt_kernel, n_chunks=tt // GLA_CHUNK, q_scale=dk ** -0.5,
                             unroll=unroll)
    return pl.pallas_call(
        kern,
        grid=(batch, GLA_HEADS, nt),
        in_specs=[
            pl.BlockSpec((tt, dk), lambda bi, h, t: (bi * nt + t, h)),
            pl.BlockSpec((tt, dk), lambda bi, h, t: (bi * nt + t, k_col0 + h)),
            pl.BlockSpec((tt, dv), lambda bi, h, t: (bi * nt + t, v_col0 + h)),
            pl.BlockSpec((tt, dv), lambda bi, h, t: (bi * nt + t, r_col0 + h)),
            pl.BlockSpec((tt, dk), lambda bi, h, t: (bi * nt + t, h)),
            pl.BlockSpec((None, 1, dv), lambda bi, h, t: (layer, 0, 0)),
        ],
        out_specs=[
            pl.BlockSpec((tt, dv), lambda bi, h, t: (bi * nt + t, h)),
            pl.BlockSpec((None, None, dk, dv), lambda bi, h, t: (bi, h, 0, 0)),
        ],
        out_shape=[
            jax.ShapeDtypeStruct((m, vw), BF16),
            jax.ShapeDtypeStruct((batch, GLA_HEADS, dk, dv), F32),
        ],
        scratch_shapes=[pltpu.VMEM((dv, dk), F32)],
        compiler_params=_params("parallel", "parallel", "arbitrary"),
        name="gla_prompt",
    )(proj, proj, proj, proj, b, head_norm.reshape(head_norm.shape[0], 1, dv))


def _gla_step_kernel(q_ref, k_ref, g_ref, v_ref, r_ref, hn_ref, s0_ref, og_ref, s_ref,
                     *, q_scale):
    s_new = jnp.exp(g_ref[...]) * s0_ref[...] + k_ref[...] * v_ref[...]
    s_ref[...] = s_new
    o = jnp.sum((q_ref[...] * q_scale) * s_new, axis=0, keepdims=True)
    r = r_ref[...]
    silu = r / (1.0 + jnp.exp(-r))
    og_ref[...] = _rms_normalize(o, hn_ref[...]) * silu


def gla_step(proj, g, head_norm, s0, layer):
    batch = proj.shape[0]
    qk = g.shape[1]
    dk = qk // GLA_HEADS
    vw = (proj.shape[1] - 2 * qk) // 2
    dv = vw // GLA_HEADS
    col = lambda a: a.reshape(batch, GLA_HEADS, dk, 1)
    row = lambda a: a.reshape(batch, GLA_HEADS, 1, dv)
    q_c = col(proj[:, :qk])
    k_c = col(proj[:, qk:2 * qk])
    g_c = col(g)
    v_r = row(proj[:, 2 * qk:2 * qk + vw])
    r_r = row(proj[:, 2 * qk + vw:])
    col_spec = pl.BlockSpec((None, None, dk, 1), lambda bi, h: (bi, h, 0, 0))
    row_spec = pl.BlockSpec((None, None, 1, dv), lambda bi, h: (bi, h, 0, 0))
    st_spec = pl.BlockSpec((None, None, dk, dv), lambda bi, h: (bi, h, 0, 0))
    og, s = pl.pallas_call(
        functools.partial(_gla_step_kernel, q_scale=dk ** -0.5),
        grid=(batch, GLA_HEADS),
        in_specs=[col_spec, col_spec, col_spec, row_spec, row_spec,
                  pl.BlockSpec((None, 1, dv), lambda bi, h: (layer, 0, 0)),
                  pl.BlockSpec((None, None, None, dk, dv), lambda bi, h: (layer, bi, h, 0, 0))],
        out_specs=[row_spec, st_spec],
        out_shape=[jax.ShapeDtypeStruct((batch, GLA_HEADS, 1, dv), F32),
                   jax.ShapeDtypeStruct(s0.shape[1:], F32)],
        compiler_params=_params("parallel", "parallel"),
        name="gla_step",
    )(q_c, k_c, g_c, v_r, r_r, head_norm.reshape(head_norm.shape[0], 1, dv), s0)
    return og.reshape(batch, vw), s


def _matmul_residual_kernel(a_ref, w_ref, h_ref, o_ref):
    o_ref[...] = h_ref[...] + jnp.dot(a_ref[...].astype(BF16), w_ref[...],
                                      preferred_element_type=F32)


def matmul_residual(a, w, h, layer, *, tm=512, tn=2048):
    m, kd = a.shape
    n = w.shape[2]
    tm = _row_tile(m, tm)
    tn = _col_tile(n, tn)
    return pl.pallas_call(
        _matmul_residual_kernel,
        grid=(m // tm, n // tn),
        in_specs=[
            pl.BlockSpec((tm, kd), lambda i, j: (i, 0)),
            pl.BlockSpec((None, kd, tn), lambda i, j: (layer, 0, j)),
            pl.BlockSpec((tm, tn), lambda i, j: (i, j)),
        ],
        out_specs=pl.BlockSpec((tm, tn), lambda i, j: (i, j)),
        out_shape=jax.ShapeDtypeStruct((m, n), F32),
        compiler_params=_params("parallel", "parallel"),
        name="matmul_residual",
    )(a, w, h)


def _mlp_kernel(h_ref, gain_ref, wu_ref, wd_ref, o_ref, xn_ref):
    @pl.when(pl.program_id(1) == 0)
    def _():
        h = h_ref[...]
        xn_ref[...] = _rms_normalize(h, gain_ref[...]).astype(xn_ref.dtype)
        o_ref[...] = h

    u = jnp.maximum(jnp.dot(xn_ref[...], wu_ref[...], preferred_element_type=F32), 0.0)
    o_ref[...] += jnp.dot((u * u).astype(BF16), wd_ref[...], preferred_element_type=F32)


def _mlp_kmean_kernel(pt_ref, h_ref, gain_ref, wu_ref, wd_ref, ck_ref, o_ref, km_ref,
                      xn_ref, pages_ref, sem_ref, *, first_block, pages_per_block, block_len):
    n_f = pl.num_programs(1)
    step = pl.program_id(0) * n_f + pl.program_id(1)
    n_steps = pl.num_programs(0) * n_f
    slot = step % 2

    def page_copies(s, sl):
        first_page = (first_block + s) * pages_per_block
        return [pltpu.make_async_copy(ck_ref.at[pt_ref[first_page + pg]], pages_ref.at[sl, pg],
                                      sem_ref.at[sl])
                for pg in range(pages_per_block)]

    @pl.when(step == 0)
    def _():
        for cp in page_copies(0, 0):
            cp.start()

    @pl.when(step + 1 < n_steps)
    def _():
        for cp in page_copies(step + 1, 1 - slot):
            cp.start()

    for cp in page_copies(step, slot):
        cp.wait()

    _mlp_kernel(h_ref, gain_ref, wu_ref, wd_ref, o_ref, xn_ref)

    total = jnp.sum(pages_ref[slot, 0], axis=0)
    for pg in range(1, pages_per_block):
        total = total + jnp.sum(pages_ref[slot, pg], axis=0)
    km_ref[...] = total / block_len


def mlp_residual(h, gain, w_up, w_down, layer, *, tm=512, tf=1024, side=None):
    m, d = h.shape
    ff = w_up.shape[2]
    tm = _row_tile(m, tm)
    tf = _col_tile(ff, tf)
    grid = (m // tm, ff // tf)
    in_specs = [
        pl.BlockSpec((tm, d), lambda i, f, *_: (i, 0)),
        pl.BlockSpec((None, 1, d), lambda i, f, *_: (layer, 0, 0)),
        pl.BlockSpec((None, d, tf), lambda i, f, *_: (layer, 0, f)),
        pl.BlockSpec((None, tf, d), lambda i, f, *_: (layer, f, 0)),
    ]
    out_spec = pl.BlockSpec((tm, d), lambda i, f, *_: (i, 0))
    out_shape = jax.ShapeDtypeStruct((m, d), F32)
    xn_scratch = pltpu.VMEM((tm, d), BF16)
    gain3 = gain.reshape(gain.shape[0], 1, d)
    if side is None:
        return pl.pallas_call(
            _mlp_kernel, grid=grid, in_specs=in_specs, out_specs=out_spec, out_shape=out_shape,
            scratch_shapes=[xn_scratch],
            compiler_params=_params("parallel", "arbitrary"),
            name="mlp_residual",
        )(h, gain3, w_up, w_down)

    cache_k, page_table_flat, first_block = side
    _, page, heads, hd = cache_k.shape
    ppb = MOBA_BLOCK // page
    n_steps = grid[0] * grid[1]
    return pl.pallas_call(
        functools.partial(_mlp_kmean_kernel, first_block=first_block, pages_per_block=ppb,
                          block_len=MOBA_BLOCK),
        grid_spec=pltpu.PrefetchScalarGridSpec(
            num_scalar_prefetch=1,
            grid=grid,
            in_specs=in_specs + [pl.BlockSpec(memory_space=pl.ANY)],
            out_specs=[out_spec,
                       pl.BlockSpec((None, heads, hd), lambda i, f, pt: (i * grid[1] + f, 0, 0))],
            scratch_shapes=[xn_scratch, pltpu.VMEM((2, ppb, page, heads, hd), F32),
                            pltpu.SemaphoreType.DMA((2,))],
        ),
        out_shape=[out_shape, jax.ShapeDtypeStruct((n_steps, heads, hd), F32)],
        compiler_params=_params("arbitrary", "arbitrary"),
        name="mlp_residual_kmean",
    )(page_table_flat, h, gain3, w_up, w_down, cache_k)


def _rmsnorm_kernel(x_ref, gain_ref, o_ref):
    o_ref[...] = _rms_normalize(x_ref[...], gain_ref[...])


def rmsnorm(x, gain, *, tm=512):
    m, d = x.shape
    tm = _row_tile(m, tm)
    return pl.pallas_call(
        _rmsnorm_kernel,
        grid=(m // tm,),
        in_specs=[pl.BlockSpec((tm, d), lambda i: (i, 0)),
                  pl.BlockSpec((1, d), lambda i: (0, 0))],
        out_specs=pl.BlockSpec((tm, d), lambda i: (i, 0)),
        out_shape=jax.ShapeDtypeStruct((m, d), F32),
        compiler_params=_params("parallel"),
        name="rmsnorm",
    )(x, gain.reshape(1, d))


def _top_blocks(gate, is_past, axis):
    n_blocks = gate.shape[axis]
    idx = lax.broadcasted_iota(jnp.int32, gate.shape, axis)
    neg_inf = jnp.float32(-jnp.inf)
    left = jnp.where(is_past, gate, neg_inf)
    chosen = jnp.zeros(gate.shape, dtype=jnp.bool_)
    picks = []
    for _ in range(MOBA_TOP_K):
        best = jnp.max(left, axis=axis, keepdims=True)
        is_best = (left == best) & (best > neg_inf)
        first = jnp.min(jnp.where(is_best, idx, n_blocks), axis=axis, keepdims=True)
        pick = idx == first
        chosen = chosen | pick
        left = jnp.where(pick, neg_inf, left)
        picks.append(first)
    return chosen, picks


def _moba_prompt_kernel(q_ref, k_ref, v_ref, o_ref, kb_ref, vt_ref, kmean_ref,
                        *, n_blocks, n_heads, hd, scale, tq):
    ti = pl.program_id(2)
    blk = MOBA_BLOCK
    heads = range(n_heads)

    @pl.when(ti == 0)
    def _():
        for j in heads:
            cols = slice(j * hd, (j + 1) * hd)
            for n in range(n_blocks):
                k_blk = k_ref[n * blk:(n + 1) * blk, cols]
                kb_ref[j, n] = k_blk.astype(BF16)
                vt_ref[j, n] = v_ref[n * blk:(n + 1) * blk, cols].T.astype(BF16)
                kmean_ref[j, n:n + 1, :] = jnp.mean(k_blk, axis=0, keepdims=True)

    q_pos = ti * tq + lax.broadcasted_iota(jnp.int32, (1, tq), 1)
    q_block = q_pos // blk
    block_id = lax.broadcasted_iota(jnp.int32, (n_blocks, tq), 0)
    key_off = lax.broadcasted_iota(jnp.int32, (blk, tq), 0)
    neg_inf = jnp.float32(-jnp.inf)

    q_bs, visible, carry0 = [], [], []
    for j in heads:
        q = q_ref[:, j * hd:(j + 1) * hd]
        gate = _contract_last(kmean_ref[j], q, precision=HIGHEST)
        chosen, _ = _top_blocks(gate, block_id < q_block, axis=0)
        visible.append(jnp.where(chosen | (block_id == q_block), 1.0, 0.0))
        q_bs.append((q * (scale * LOG2_E)).astype(BF16))
        carry0.append((jnp.full((1, tq), neg_inf), jnp.zeros((1, tq), F32),
                       jnp.zeros((hd, tq), F32)))

    def key_block(n, carry):
        q_off = q_pos - n * blk
        raw = [_contract_last(kb_ref[j, n], q_bs[j]) for j in heads]
        partial = []
        for j in heads:
            m_run, l_run, _ = carry[j]
            vis = jnp.max(jnp.where(block_id == n, visible[j], 0.0), axis=0, keepdims=True)
            threshold = jnp.where(vis > 0.0, q_off, -1)
            s = jnp.where(key_off <= threshold, raw[j], neg_inf)
            m_new = jnp.maximum(m_run, jnp.max(s, axis=0, keepdims=True))
            seen = m_new > neg_inf
            m_safe = jnp.where(seen, m_new, 0.0)
            alpha = jnp.where(seen, jnp.exp2(m_run - m_safe), 0.0)
            p = jnp.exp2(s - m_safe)
            l_new = alpha * l_run + jnp.sum(p, axis=0, keepdims=True)
            partial.append((m_new, l_new, alpha, p.astype(BF16)))
        out = []
        for j in heads:
            m_new, l_new, alpha, p_b = partial[j]
            acc = alpha * carry[j][2] + jnp.dot(vt_ref[j, n], p_b,
                                                preferred_element_type=F32)
            out.append((m_new, l_new, acc))
        return tuple(out)

    final = lax.fori_loop(0, (ti + 1) * (tq // blk), key_block, tuple(carry0))
    for j in heads:
        _, l_fin, acc = final[j]
        o_ref[:, j * hd:(j + 1) * hd] = (acc / l_fin).T.astype(o_ref.dtype)


def moba_prompt(q, k, v, *, batch, seq, heads_per_step=4, tq=512):
    m, width = q.shape
    hd = width // MOBA_HEADS
    tq = min(tq, seq)
    assert seq % tq == 0 and tq % MOBA_BLOCK == 0 and MOBA_HEADS % heads_per_step == 0
    n_blocks = seq // MOBA_BLOCK
    n_tiles = seq // tq
    gw = heads_per_step * hd
    kern = functools.partial(_moba_prompt_kernel, n_blocks=n_blocks, n_heads=heads_per_step,
                             hd=hd, scale=hd ** -0.5, tq=tq)
    return pl.pallas_call(
        kern,
        grid=(batch, MOBA_HEADS // heads_per_step, n_tiles),
        in_specs=[
            pl.BlockSpec((tq, gw), lambda b, h, i: (b * n_tiles + i, h)),
            pl.BlockSpec((seq, gw), lambda b, h, i: (b, h)),
            pl.BlockSpec((seq, gw), lambda b, h, i: (b, h)),
        ],
        out_specs=pl.BlockSpec((tq, gw), lambda b, h, i: (b * n_tiles + i, h)),
        out_shape=jax.ShapeDtypeStruct((m, width), BF16),
        scratch_shapes=[pltpu.VMEM((heads_per_step, n_blocks, MOBA_BLOCK, hd), BF16),
                        pltpu.VMEM((heads_per_step, n_blocks, hd, MOBA_BLOCK), BF16),
                        pltpu.VMEM((heads_per_step, n_blocks, hd), F32)],
        compiler_params=_params("parallel", "parallel", "arbitrary"),
        name="moba_prompt",
    )(q, k, v)


def _decode_select_kernel(q_ref, kmean_ref, o_ref):
    gate = jnp.sum(kmean_ref[...] * q_ref[...], axis=-1, keepdims=True)
    _, picks = _top_blocks(gate, jnp.full(gate.shape, True), axis=0)
    for r, first in enumerate(picks):
        o_ref[r] = first[0]


def decode_select(q, kmean):
    batch, n_past, heads, hd = kmean.shape
    assert n_past >= MOBA_TOP_K
    return pl.pallas_call(
        _decode_select_kernel,
        grid=(batch,),
        in_specs=[pl.BlockSpec((None, heads, hd), lambda b: (b, 0, 0)),
                  pl.BlockSpec((None, n_past, heads, hd), lambda b: (b, 0, 0, 0))],
        out_specs=pl.BlockSpec((None, MOBA_TOP_K, heads, 1), lambda b: (b, 0, 0, 0)),
        out_shape=jax.ShapeDtypeStruct((batch, MOBA_TOP_K, heads, 1), jnp.int32),
        compiler_params=_params("parallel"),
        name="moba_decode_select",
    )(q, kmean)


def _decode_attend_kernel(pt_ref, top_ref, q_ref, kn_ref, vn_ref, ck_ref, cv_ref, o_ref,
                          kbuf_ref, vbuf_ref, sem_ref, *, heads, pages_per_block, n_pages_seq,
                          scale):
    b = pl.program_id(0)
    n_slots = MOBA_TOP_K * pages_per_block

    def head_copies(h):
        copies = []
        for r in range(MOBA_TOP_K):
            block = top_ref[(b * MOBA_TOP_K + r) * heads + h]
            for pg in range(pages_per_block):
                page = pt_ref[b * n_pages_seq + block * pages_per_block + pg]
                slot = r * pages_per_block + pg
                copies.append(pltpu.make_async_copy(
                    ck_ref.at[page, :, h, :], kbuf_ref.at[h, slot], sem_ref.at[0, h]))
                copies.append(pltpu.make_async_copy(
                    cv_ref.at[page, :, h, :], vbuf_ref.at[h, slot], sem_ref.at[1, h]))
        return copies

    all_copies = [head_copies(h) for h in range(heads)]
    for copies in all_copies:
        for cp in copies:
            cp.start()

    for h in range(heads):
        for cp in all_copies[h]:
            cp.wait()
        q = q_ref[h:h + 1, :]
        k_new, v_new = kn_ref[h:h + 1, :], vn_ref[h:h + 1, :]
        s_own = jnp.sum(q * k_new, axis=-1, keepdims=True) * scale
        s_past = [jnp.sum(kbuf_ref[h, i] * q, axis=-1, keepdims=True) * scale
                  for i in range(n_slots)]
        m = s_own
        for s in s_past:
            m = jnp.maximum(m, jnp.max(s, axis=0, keepdims=True))
        p_own = jnp.exp(s_own - m)
        denom = p_own
        acc = p_own * v_new
        for i, s in enumerate(s_past):
            p = jnp.exp(s - m)
            denom = denom + jnp.sum(p, axis=0, keepdims=True)
            acc = acc + jnp.sum(p * vbuf_ref[h, i], axis=0, keepdims=True)
        o_ref[h:h + 1, :] = acc / denom


def decode_attend(q, k_new, v_new, cache_k, cache_v, page_table_flat, top_flat, *, n_pages_seq):
    batch, heads, hd = q.shape
    page = cache_k.shape[1]
    ppb = MOBA_BLOCK // page
    n_slots = MOBA_TOP_K * ppb
    tok_spec = pl.BlockSpec((None, heads, hd), lambda b, pt, tp: (b, 0, 0))
    hbm_spec = pl.BlockSpec(memory_space=pl.ANY)
    return pl.pallas_call(
        functools.partial(_decode_attend_kernel, heads=heads, pages_per_block=ppb,
                          n_pages_seq=n_pages_seq, scale=hd ** -0.5),
        grid_spec=pltpu.PrefetchScalarGridSpec(
            num_scalar_prefetch=2,
            grid=(batch,),
            in_specs=[tok_spec, tok_spec, tok_spec, hbm_spec, hbm_spec],
            out_specs=tok_spec,
            scratch_shapes=[pltpu.VMEM((heads, n_slots, page, hd), F32),
                            pltpu.VMEM((heads, n_slots, page, hd), F32),
                            pltpu.SemaphoreType.DMA((2, heads))],
        ),
        out_shape=jax.ShapeDtypeStruct((batch, heads, hd), F32),
        compiler_params=_params("arbitrary"),
        name="moba_decode_attend",
    )(page_table_flat, top_flat, q, k_new, v_new, cache_k, cache_v)


def _trunk(x, weights, *, batch, seq, gla_s0=None, cache=None, kmean=None, kmean_side=None):
    w = weights
    n_gla = w["w_in"].shape[0]
    depth = w["w_up"].shape[0]
    width = w["w_kv"].shape[2] // 2
    heads = MOBA_HEADS
    hd = width // heads
    h = x
    states = []
    kmean_parts = []
    k_new = v_new = None
    for l in range(depth):
        if l < n_gla:
            proj, = norm_matmul(h, w["norm_mix"][l], w["w_in"], l, n_cols=w["n_qkvr"], tn=1536)
            b = gla_gate(h, w["norm_mix"][l], w["w_z"], w["w_gate"], w["b_gate"], l,
                         chunk=min(GLA_CHUNK, seq))
            if cache is None:
                og, s = gla_prompt(proj, b, w["head_norm"], l, batch=batch, seq=seq)
            else:
                og, s = gla_step(proj, b, w["head_norm"], gla_s0, l)
            states.append(s)
            h = matmul_residual(og, w["w_out"], h, l)
        else:
            i = l - n_gla
            q, = norm_matmul(h, w["norm_mix"][l], w["w_q"], i, tm=512, tn=2048)
            if cache is None:
                o = moba_prompt(q, k_new, v_new, batch=batch, seq=seq)
            else:
                cache_k, cache_v, page_table_flat, n_pages = cache
                q3 = q.reshape(batch, heads, hd)
                top = decode_select(q3, kmean)
                o = decode_attend(q3, k_new.reshape(batch, heads, hd),
                                  v_new.reshape(batch, heads, hd), cache_k, cache_v,
                                  page_table_flat, top.reshape(-1),
                                  n_pages_seq=n_pages).reshape(batch, width)
            h = matmul_residual(o, w["w_o"], h, i)
        if kmean_side is None:
            h = mlp_residual(h, w["norm_mlp"], w["w_up"], w["w_down"], l)
        else:
            first_block = sum(part.shape[0] for part in kmean_parts)
            h, part = mlp_residual(h, w["norm_mlp"], w["w_up"], w["w_down"], l,
                                   side=(*kmean_side, first_block))
            kmean_parts.append(part)
        if l == n_gla - 1:
            k_new, v_new = norm_matmul(h, w["kv_norm"], w["w_kv"], 0, n_split=2, tn=512)
    y = rmsnorm(h, w["norm_final"])
    return y, jnp.stack(states), k_new, v_new, kmean_parts


def kernel(x_prompt, x_sample, state_gla, cache_k, cache_v, page_table, norm_mix, norm_mlp,
           w_mlp_up, w_mlp_down, w_in_a, w_gate_a, b_gate_a, head_norm_a, w_out_a, kv_norm,
           w_kv, w_q_b, w_o_b, norm_final):
    batch, seq, d = x_prompt.shape
    dec_batch, dec_seq, _ = x_sample.shape
    assert dec_seq == 1
    n_qkvr = w_in_a.shape[2] - GLA_GATE_RANK
    weights = dict(
        norm_mix=norm_mix, norm_mlp=norm_mlp, norm_final=norm_final, kv_norm=kv_norm,
        head_norm=head_norm_a, w_gate=w_gate_a, b_gate=b_gate_a,
        w_in=w_in_a.astype(BF16), n_qkvr=n_qkvr,
        w_z=w_in_a[:, :, n_qkvr:],
        w_out=w_out_a.astype(BF16),
        w_up=w_mlp_up.astype(BF16),
        w_down=w_mlp_down.astype(BF16),
        w_kv=w_kv.astype(BF16)[None],
        w_q=w_q_b.astype(BF16),
        w_o=w_o_b.astype(BF16),
    )
    heads, hd = cache_k.shape[2], cache_k.shape[3]

    page_table_flat = page_table.reshape(-1)
    n_pages = page_table.shape[1]
    n_past = n_pages * cache_k.shape[1] // MOBA_BLOCK
    y_p, s_p, k_p, v_p, kmean_parts = _trunk(
        x_prompt.reshape(batch * seq, d), weights, batch=batch, seq=seq,
        kmean_side=(cache_k, page_table_flat))
    kmean = jnp.concatenate(kmean_parts, axis=0)
    assert kmean.shape[0] == dec_batch * n_past, "prompt MLP grid steps must cover the cached blocks"
    cache = (cache_k, cache_v, page_table_flat, n_pages)
    y_s, s_s, k_s, v_s, _ = _trunk(
        x_sample.reshape(dec_batch, d), weights, batch=dec_batch, seq=1, gla_s0=state_gla,
        cache=cache, kmean=kmean.reshape(dec_batch, n_past, heads, hd))
    return (y_p.reshape(batch, seq, d), y_s.reshape(dec_batch, 1, d), s_p, s_s,
            k_p.reshape(batch, seq, heads, hd), v_p.reshape(batch, seq, heads, hd),
            k_s.reshape(dec_batch, 1, heads, hd), v_s.reshape(dec_batch, 1, heads, hd))
```

```python
import functools

import jax
import jax.numpy as jnp
from jax import lax
from jax.experimental import pallas as pl
from jax.experimental.pallas import tpu as pltpu

F32 = jnp.float32
BF16 = jnp.bfloat16
HIGHEST = lax.Precision.HIGHEST

RMS_EPS = 1e-6
GLA_HEADS = 4
GLA_GATE_RANK = 16
GLA_GATE_TAU = 16.0
GLA_CHUNK = 64
GLA_SUB = 16
MOBA_HEADS = 16
MOBA_BLOCK = 256
MOBA_TOP_K = 3
LOG2_E = 1.4426950408889634

V7X_VMEM_BYTES = 64 * 1024 * 1024
VMEM_LIMIT_BYTES = V7X_VMEM_BYTES - 8 * 1024 * 1024
BF16_SUBLANES = 16


def _params(*semantics):
    return pltpu.CompilerParams(dimension_semantics=semantics,
                                vmem_limit_bytes=VMEM_LIMIT_BYTES)


def _rms_normalize(x, gain):
    ms = jnp.mean(x * x, axis=-1, keepdims=True)
    return x * lax.rsqrt(ms + RMS_EPS) * gain


def _row_tile(m, want):
    return want if m % want == 0 else m


def _col_tile(n, want):
    while n % want:
        want //= 2
    return want


def _contract_last(a, b, **kw):
    return lax.dot_general(a, b, (((1,), (1,)), ((), ())), preferred_element_type=F32, **kw)


def _contract_first(a, b):
    return lax.dot_general(a, b, (((0,), (0,)), ((), ())), preferred_element_type=F32)


def _norm_matmul_kernel(x_ref, gain_ref, *refs, n_split):
    w_refs, o_refs, xn_ref = refs[:n_split], refs[n_split:2 * n_split], refs[2 * n_split]

    @pl.when(pl.program_id(1) == 0)
    def _():
        xn_ref[...] = _rms_normalize(x_ref[...], gain_ref[...]).astype(xn_ref.dtype)

    for w_ref, o_ref in zip(w_refs, o_refs):
        o_ref[...] = jnp.dot(xn_ref[...], w_ref[...], preferred_element_type=F32)


def norm_matmul(x, gain, w, layer, *, n_cols=None, n_split=1, tm=1024, tn=1024):
    m, d = x.shape
    n = w.shape[2] if n_cols is None else n_cols
    n_out = n // n_split
    tm = _row_tile(m, tm)
    tn = _col_tile(n_out, tn)
    per_split = n_out // tn

    def w_spec(s):
        return pl.BlockSpec((None, d, tn), lambda i, j: (layer, 0, s * per_split + j))

    return pl.pallas_call(
        functools.partial(_norm_matmul_kernel, n_split=n_split),
        grid=(m // tm, per_split),
        in_specs=[pl.BlockSpec((tm, d), lambda i, j: (i, 0)),
                  pl.BlockSpec((1, d), lambda i, j: (0, 0))]
                 + [w_spec(s) for s in range(n_split)],
        out_specs=[pl.BlockSpec((tm, tn), lambda i, j: (i, j))] * n_split,
        out_shape=[jax.ShapeDtypeStruct((m, n_out), F32)] * n_split,
        scratch_shapes=[pltpu.VMEM((tm, d), BF16)],
        compiler_params=_params("parallel", "arbitrary"),
        name="norm_matmul",
    )(x, gain.reshape(1, d), *([w] * n_split))


def _split_bf16(a):
    hi = a.astype(BF16)
    return hi, a - hi.astype(F32)


def _dot_split(a, b):
    a_hi, a_rest = _split_bf16(a)
    b_hi, b_rest = _split_bf16(b)
    a_lo, b_lo = a_rest.astype(BF16), b_rest.astype(BF16)
    return (jnp.dot(a_hi, b_hi, preferred_element_type=F32)
            + jnp.dot(a_lo, b_hi, preferred_element_type=F32)
            + jnp.dot(a_hi, b_lo, preferred_element_type=F32))


def _gla_gate_kernel(x_ref, gain_ref, wz_ref, wg_ref, bg_ref, o_ref, *, chunk):
    xn = _rms_normalize(x_ref[...], gain_ref[...])
    z = _dot_split(xn, wz_ref[...])
    logit = _dot_split(z, wg_ref[...]) + bg_ref[...]
    log_sig = jnp.minimum(logit, 0.0) - jnp.log(1.0 + jnp.exp(-jnp.abs(logit)))
    g = log_sig / GLA_GATE_TAU
    if chunk > 1:
        rows = g.shape[0]
        row = lax.broadcasted_iota(jnp.int32, (rows, rows), 0)
        col = lax.broadcasted_iota(jnp.int32, (rows, rows), 1)
        same_chunk_past = (row >= col) & (row // chunk == col // chunk)
        ones = jnp.where(same_chunk_past, 1.0, 0.0).astype(BF16)
        g_hi, g_rest = _split_bf16(g)
        g_mid, g_rest = _split_bf16(g_rest)
        g = (jnp.dot(ones, g_hi, preferred_element_type=F32)
             + jnp.dot(ones, g_mid, preferred_element_type=F32)
             + jnp.dot(ones, g_rest.astype(BF16), preferred_element_type=F32))
    o_ref[...] = g


def gla_gate(x, gain, w_z, w_gate, b_gate, layer, *, chunk, tm=256):
    m, d = x.shape
    rank, n = w_gate.shape[1:]
    tm = _row_tile(m, tm)
    assert chunk == 1 or tm % chunk == 0
    return pl.pallas_call(
        functools.partial(_gla_gate_kernel, chunk=chunk),
        grid=(m // tm,),
        in_specs=[
            pl.BlockSpec((tm, d), lambda i: (i, 0)),
            pl.BlockSpec((1, d), lambda i: (0, 0)),
            pl.BlockSpec((None, d, rank), lambda i: (layer, 0, 0)),
            pl.BlockSpec((None, rank, n), lambda i: (layer, 0, 0)),
            pl.BlockSpec((None, 1, n), lambda i: (layer, 0, 0)),
        ],
        out_specs=pl.BlockSpec((tm, n), lambda i: (i, 0)),
        out_shape=jax.ShapeDtypeStruct((m, n), F32),
        compiler_params=_params("parallel"),
        name="gla_gate",
    )(x, gain.reshape(1, d), w_z, w_gate, b_gate.reshape(b_gate.shape[0], 1, n))


def _gla_chunk(q, k, v, b, st):
    c, dk = q.shape
    n_sub = c // GLA_SUB
    row = lax.broadcasted_iota(jnp.int32, (c, c), 0)
    col = lax.broadcasted_iota(jnp.int32, (c, c), 1)

    def rows_of(r, n):
        return jnp.concatenate(
            [jnp.broadcast_to(b[j * GLA_SUB + r:j * GLA_SUB + r + 1, :], (GLA_SUB, dk))
             for j in range(n)], axis=0)

    b_end = b[c - 1:c, :]
    v_b = v.astype(BF16)

    k_end = (k * jnp.exp(b_end - b)).astype(BF16)
    st_new = st * jnp.exp(b_end) + _contract_first(v_b, k_end)

    from_start = b - rows_of(0, n_sub)
    q_diag = q * jnp.exp(from_start)
    k_diag = k * jnp.exp(-from_start)
    att_diag = _contract_last(q_diag.astype(BF16), k_diag.astype(BF16))
    same_sub = (row // GLA_SUB) == (col // GLA_SUB)
    att = jnp.where(same_sub & (row >= col), att_diag, 0.0)

    if n_sub > 1:
        past = c - GLA_SUB
        k_off = (k[:past] * jnp.exp(rows_of(GLA_SUB - 1, n_sub - 1) - b[:past])).astype(BF16)
        q_parts, k_parts = [], []
        for j in range(n_sub - 1):
            lo = (j + 1) * GLA_SUB
            q_j = (q[lo:] * jnp.exp(b[lo:] - b[lo - 1:lo, :])).astype(BF16)
            q_parts.append(jnp.concatenate([jnp.zeros((lo, dk), BF16), q_j], axis=0))
            pieces = [k_off[lo - GLA_SUB:lo], jnp.zeros((c - lo, dk), BF16)]
            if j:
                pieces.insert(0, jnp.zeros((lo - GLA_SUB, dk), BF16))
            k_parts.append(jnp.concatenate(pieces, axis=0))
        att = att + _contract_last(jnp.concatenate(q_parts, axis=1),
                                   jnp.concatenate(k_parts, axis=1))

    o = _contract_last((q * jnp.exp(b)).astype(BF16), st.astype(BF16))
    o = o + jnp.dot(att.astype(BF16), v_b, preferred_element_type=F32)
    return o, st_new


def _gla_prompt_kernel(q_ref, k_ref, v_ref, r_ref, b_ref, hn_ref, og_ref, s_ref, st_ref,
                       *, n_chunks, q_scale, unroll):
    t = pl.program_id(2)

    @pl.when(t == 0)
    def _():
        st_ref[...] = jnp.zeros_like(st_ref)

    def chunk(ci, carry):
        rows = pl.ds(pl.multiple_of(ci * GLA_CHUNK, GLA_CHUNK), GLA_CHUNK)
        o, st_new = _gla_chunk(q_ref[rows, :] * q_scale, k_ref[rows, :], v_ref[rows, :],
                               b_ref[rows, :], st_ref[...])
        st_ref[...] = st_new
        r = r_ref[rows, :]
        silu = r / (1.0 + jnp.exp(-r))
        og_ref[rows, :] = (_rms_normalize(o, hn_ref[...]) * silu).astype(og_ref.dtype)
        return carry

    lax.fori_loop(0, n_chunks, chunk, 0, unroll=unroll)

    @pl.when(t == pl.num_programs(2) - 1)
    def _():
        s_ref[...] = st_ref[...].T


def gla_prompt(proj, b, head_norm, layer, *, batch, seq, tt=512, unroll=4):
    m = batch * seq
    qk = b.shape[1]
    dk = qk // GLA_HEADS
    vw = (proj.shape[1] - 2 * qk) // 2
    dv = vw // GLA_HEADS
    tt = min(tt, seq)
    assert seq % tt == 0 and tt % GLA_CHUNK == 0 and dv % dk == 0
    nt = seq // tt
    k_col0 = qk // dk
    v_col0 = 2 * qk // dv
    r_col0 = (2 * qk + vw) // dv
    kern = functools.partial(_gla_prompt_kernel, n_chunks=tt // GLA_CHUNK, q_scale=dk ** -0.5,
                             unroll=unroll)
    return pl.pallas_call(
        kern,
        grid=(batch, GLA_HEADS, nt),
        in_specs=[
            pl.BlockSpec((tt, dk), lambda bi, h, t: (bi * nt + t, h)),
            pl.BlockSpec((tt, dk), lambda bi, h, t: (bi * nt + t, k_col0 + h)),
            pl.BlockSpec((tt, dv), lambda bi, h, t: (bi * nt + t, v_col0 + h)),
            pl.BlockSpec((tt, dv), lambda bi, h, t: (bi * nt + t, r_col0 + h)),
            pl.BlockSpec((tt, dk), lambda bi, h, t: (bi * nt + t, h)),
            pl.BlockSpec((None, 1, dv), lambda bi, h, t: (layer, 0, 0)),
        ],
        out_specs=[
            pl.BlockSpec((tt, dv), lambda bi, h, t: (bi * nt + t, h)),
            pl.BlockSpec((None, None, dk, dv), lambda bi, h, t: (bi, h, 0, 0)),
        ],
        out_shape=[
            jax.ShapeDtypeStruct((m, vw), BF16),
            jax.ShapeDtypeStruct((batch, GLA_HEADS, dk, dv), F32),
        ],
        scratch_shapes=[pltpu.VMEM((dv, dk), F32)],
        compiler_params=_params("parallel", "parallel", "arbitrary"),
        name="gla_prompt",
    )(proj, proj, proj, proj, b, head_norm.reshape(head_norm.shape[0], 1, dv))


def _gla_step_kernel(q_ref, k_ref, g_ref, v_ref, r_ref, hn_ref, s0_ref, og_ref, s_ref,
                     *, q_scale):
    s_new = jnp.exp(g_ref[...]) * s0_ref[...] + k_ref[...] * v_ref[...]
    s_ref[...] = s_new
    o = jnp.sum((q_ref[...] * q_scale) * s_new, axis=0, keepdims=True)
    r = r_ref[...]
    silu = r / (1.0 + jnp.exp(-r))
    og_ref[...] = _rms_normalize(o, hn_ref[...]) * silu


def gla_step(proj, g, head_norm, s0, layer):
    batch = proj.shape[0]
    qk = g.shape[1]
    dk = qk // GLA_HEADS
    vw = (proj.shape[1] - 2 * qk) // 2
    dv = vw // GLA_HEADS
    col = lambda a: a.reshape(batch, GLA_HEADS, dk, 1)
    row = lambda a: a.reshape(batch, GLA_HEADS, 1, dv)
    q_c = col(proj[:, :qk])
    k_c = col(proj[:, qk:2 * qk])
    g_c = col(g)
    v_r = row(proj[:, 2 * qk:2 * qk + vw])
    r_r = row(proj[:, 2 * qk + vw:])
    col_spec = pl.BlockSpec((None, None, dk, 1), lambda bi, h: (bi, h, 0, 0))
    row_spec = pl.BlockSpec((None, None, 1, dv), lambda bi, h: (bi, h, 0, 0))
    st_spec = pl.BlockSpec((None, None, dk, dv), lambda bi, h: (bi, h, 0, 0))
    og, s = pl.pallas_call(
        functools.partial(_gla_step_kernel, q_scale=dk ** -0.5),
        grid=(batch, GLA_HEADS),
        in_specs=[col_spec, col_spec, col_spec, row_spec, row_spec,
                  pl.BlockSpec((None, 1, dv), lambda bi, h: (layer, 0, 0)),
                  pl.BlockSpec((None, None, None, dk, dv), lambda bi, h: (layer, bi, h, 0, 0))],
        out_specs=[row_spec, st_spec],
        out_shape=[jax.ShapeDtypeStruct((batch, GLA_HEADS, 1, dv), F32),
                   jax.ShapeDtypeStruct(s0.shape[1:], F32)],
        compiler_params=_params("parallel", "parallel"),
        name="gla_step",
    )(q_c, k_c, g_c, v_r, r_r, head_norm.reshape(head_norm.shape[0], 1, dv), s0)
    return og.reshape(batch, vw), s


def _matmul_residual_kernel(a_ref, w_ref, h_ref, o_ref):
    o_ref[...] = h_ref[...] + jnp.dot(a_ref[...].astype(BF16), w_ref[...],
                                      preferred_element_type=F32)


def matmul_residual(a, w, h, layer, *, tm=512, tn=2048):
    m, kd = a.shape
    n = w.shape[2]
    tm = _row_tile(m, tm)
    tn = _col_tile(n, tn)
    return pl.pallas_call(
        _matmul_residual_kernel,
        grid=(m // tm, n // tn),
        in_specs=[
            pl.BlockSpec((tm, kd), lambda i, j: (i, 0)),
            pl.BlockSpec((None, kd, tn), lambda i, j: (layer, 0, j)),
            pl.BlockSpec((tm, tn), lambda i, j: (i, j)),
        ],
        out_specs=pl.BlockSpec((tm, tn), lambda i, j: (i, j)),
        out_shape=jax.ShapeDtypeStruct((m, n), F32),
        compiler_params=_params("parallel", "parallel"),
        name="matmul_residual",
    )(a, w, h)


def _mlp_kernel(*refs, has_final_gain, n_casts, kmean):
    refs = list(refs)
    pt_ref = refs.pop(0) if kmean else None
    h_ref, gain_ref, wu_ref, wd_ref = refs[:4]
    del refs[:4]
    final_gain_ref = refs.pop(0) if has_final_gain else None
    cast_src = [refs.pop(0) for _ in range(n_casts)]
    ck_ref = refs.pop(0) if kmean else None
    o_ref = refs.pop(0)
    km_ref = refs.pop(0) if kmean else None
    cast_dst = [refs.pop(0) for _ in range(n_casts)]
    xn_ref = refs.pop(0)

    f, n_f = pl.program_id(1), pl.num_programs(1)
    if kmean:
        first_block, pages_per_block, block_len = kmean
        pages_ref, sem_ref = refs
        step = pl.program_id(0) * n_f + f
        slot = step % 2

        def page_copies(s, sl):
            first_page = (first_block + s) * pages_per_block
            return [pltpu.make_async_copy(ck_ref.at[pt_ref[first_page + pg]],
                                          pages_ref.at[sl, pg], sem_ref.at[sl])
                    for pg in range(pages_per_block)]

        @pl.when(step == 0)
        def _():
            for cp in page_copies(0, 0):
                cp.start()

        @pl.when(step + 1 < pl.num_programs(0) * n_f)
        def _():
            for cp in page_copies(step + 1, 1 - slot):
                cp.start()

        for cp in page_copies(step, slot):
            cp.wait()

    @pl.when(f == 0)
    def _():
        h = h_ref[...]
        xn_ref[...] = _rms_normalize(h, gain_ref[...]).astype(xn_ref.dtype)
        o_ref[...] = h

    u = jnp.maximum(jnp.dot(xn_ref[...], wu_ref[...], preferred_element_type=F32), 0.0)
    o_ref[...] += jnp.dot((u * u).astype(BF16), wd_ref[...], preferred_element_type=F32)

    if has_final_gain:
        @pl.when(f == n_f - 1)
        def _():
            o_ref[...] = _rms_normalize(o_ref[...], final_gain_ref[...])

    for src, dst in zip(cast_src, cast_dst):
        dst[...] = src[...].astype(dst.dtype)

    if kmean:
        total = jnp.sum(pages_ref[slot, 0], axis=0)
        for pg in range(1, pages_per_block):
            total = total + jnp.sum(pages_ref[slot, pg], axis=0)
        km_ref[...] = total / block_len


def mlp_residual(h, gain, w_up, w_down, layer, *, tm=512, tf=1024, final_gain=None,
                 casts=(), kmean_side=None):
    m, d = h.shape
    ff = w_up.shape[2]
    tm = _row_tile(m, tm)
    tf = _col_tile(ff, tf)
    grid = (m // tm, ff // tf)
    n_steps = grid[0] * grid[1]
    step_of = lambda i, f: i * grid[1] + f

    operands = [h, gain.reshape(gain.shape[0], 1, d), w_up, w_down]
    in_specs = [
        pl.BlockSpec((tm, d), lambda i, f, *_: (i, 0)),
        pl.BlockSpec((None, 1, d), lambda i, f, *_: (layer, 0, 0)),
        pl.BlockSpec((None, d, tf), lambda i, f, *_: (layer, 0, f)),
        pl.BlockSpec((None, tf, d), lambda i, f, *_: (layer, f, 0)),
    ]
    out_specs = [pl.BlockSpec((tm, d), lambda i, f, *_: (i, 0))]
    out_shapes = [jax.ShapeDtypeStruct((m, d), F32)]
    scratch = [pltpu.VMEM((tm, d), BF16)]
    if final_gain is not None:
        operands.append(final_gain.reshape(1, d))
        in_specs.append(pl.BlockSpec((1, d), lambda i, f, *_: (0, 0)))
    cast_out_specs, cast_out_shapes = [], []
    for src, src_layer in casts:
        _, rows, cols = src.shape
        slab = rows // n_steps
        assert rows == slab * n_steps and slab % BF16_SUBLANES == 0
        operands.append(src)
        in_specs.append(pl.BlockSpec((None, slab, cols),
                                     lambda i, f, *_, src_layer=src_layer: (src_layer, step_of(i, f), 0)))
        cast_out_specs.append(pl.BlockSpec((None, slab, cols),
                                           lambda i, f, *_: (0, step_of(i, f), 0)))
        cast_out_shapes.append(jax.ShapeDtypeStruct((1, rows, cols), BF16))
    kmean = None
    if kmean_side is not None:
        cache_k, page_table_flat, first_block = kmean_side
        _, page, heads, hd = cache_k.shape
        ppb = MOBA_BLOCK // page
        kmean = (first_block, ppb, MOBA_BLOCK)
        operands = [page_table_flat] + operands + [cache_k]
        in_specs.append(pl.BlockSpec(memory_space=pl.ANY))
        out_specs.append(pl.BlockSpec((None, heads, hd), lambda i, f, *_: (step_of(i, f), 0, 0)))
        out_shapes.append(jax.ShapeDtypeStruct((n_steps, heads, hd), F32))
        scratch += [pltpu.VMEM((2, ppb, page, heads, hd), F32), pltpu.SemaphoreType.DMA((2,))]
    kern = functools.partial(_mlp_kernel, has_final_gain=final_gain is not None,
                             n_casts=len(casts), kmean=kmean)
    return pl.pallas_call(
        kern,
        grid_spec=pltpu.PrefetchScalarGridSpec(
            num_scalar_prefetch=1 if kmean else 0, grid=grid, in_specs=in_specs,
            out_specs=out_specs + cast_out_specs, scratch_shapes=scratch),
        out_shape=out_shapes + cast_out_shapes,
        compiler_params=_params("arbitrary" if kmean else "parallel", "arbitrary"),
        name="mlp_residual",
    )(*operands)


def _top_blocks(gate, is_past, axis):
    n_blocks = gate.shape[axis]
    idx = lax.broadcasted_iota(jnp.int32, gate.shape, axis)
    neg_inf = jnp.float32(-jnp.inf)
    left = jnp.where(is_past, gate, neg_inf)
    chosen = jnp.zeros(gate.shape, dtype=jnp.bool_)
    picks = []
    for _ in range(MOBA_TOP_K):
        best = jnp.max(left, axis=axis, keepdims=True)
        is_best = (left == best) & (best > neg_inf)
        first = jnp.min(jnp.where(is_best, idx, n_blocks), axis=axis, keepdims=True)
        pick = idx == first
        chosen = chosen | pick
        left = jnp.where(pick, neg_inf, left)
        picks.append(first)
    return chosen, picks


def _moba_prompt_kernel(q_ref, k_ref, v_ref, o_ref, kb_ref, vt_ref, kmean_ref,
                        *, n_blocks, n_heads, hd, scale, tq):
    ti = pl.program_id(2)
    blk = MOBA_BLOCK
    heads = range(n_heads)

    @pl.when(ti == 0)
    def _():
        for j in heads:
            cols = slice(j * hd, (j + 1) * hd)
            for n in range(n_blocks):
                k_blk = k_ref[n * blk:(n + 1) * blk, cols]
                kb_ref[j, n] = k_blk.astype(BF16)
                vt_ref[j, n] = v_ref[n * blk:(n + 1) * blk, cols].T.astype(BF16)
                kmean_ref[j, n:n + 1, :] = jnp.mean(k_blk, axis=0, keepdims=True)

    q_pos = ti * tq + lax.broadcasted_iota(jnp.int32, (1, tq), 1)
    q_block = q_pos // blk
    block_id = lax.broadcasted_iota(jnp.int32, (n_blocks, tq), 0)
    key_off = lax.broadcasted_iota(jnp.int32, (blk, tq), 0)
    neg_inf = jnp.float32(-jnp.inf)

    q_bs, visible, carry0 = [], [], []
    for j in heads:
        q = q_ref[:, j * hd:(j + 1) * hd]
        gate = _contract_last(kmean_ref[j], q, precision=HIGHEST)
        chosen, _ = _top_blocks(gate, block_id < q_block, axis=0)
        visible.append(jnp.where(chosen | (block_id == q_block), 1.0, 0.0))
        q_bs.append((q * (scale * LOG2_E)).astype(BF16))
        carry0.append((jnp.full((1, tq), neg_inf), jnp.zeros((1, tq), F32),
                       jnp.zeros((hd, tq), F32)))

    def key_block(n, carry):
        q_off = q_pos - n * blk
        raw = [_contract_last(kb_ref[j, n], q_bs[j]) for j in heads]
        partial = []
        for j in heads:
            m_run, l_run, _ = carry[j]
            vis = jnp.max(jnp.where(block_id == n, visible[j], 0.0), axis=0, keepdims=True)
            threshold = jnp.where(vis > 0.0, q_off, -1)
            s = jnp.where(key_off <= threshold, raw[j], neg_inf)
            m_new = jnp.maximum(m_run, jnp.max(s, axis=0, keepdims=True))
            seen = m_new > neg_inf
            m_safe = jnp.where(seen, m_new, 0.0)
            alpha = jnp.where(seen, jnp.exp2(m_run - m_safe), 0.0)
            p = jnp.exp2(s - m_safe)
            l_new = alpha * l_run + jnp.sum(p, axis=0, keepdims=True)
            partial.append((m_new, l_new, alpha, p.astype(BF16)))
        out = []
        for j in heads:
            m_new, l_new, alpha, p_b = partial[j]
            acc = alpha * carry[j][2] + jnp.dot(vt_ref[j, n], p_b,
                                                preferred_element_type=F32)
            out.append((m_new, l_new, acc))
        return tuple(out)

    final = lax.fori_loop(0, (ti + 1) * (tq // blk), key_block, tuple(carry0))
    for j in heads:
        _, l_fin, acc = final[j]
        o_ref[:, j * hd:(j + 1) * hd] = (acc / l_fin).T.astype(o_ref.dtype)


def moba_prompt(q, k, v, *, batch, seq, heads_per_step=4, tq=512):
    m, width = q.shape
    hd = width // MOBA_HEADS
    tq = min(tq, seq)
    assert seq % tq == 0 and tq % MOBA_BLOCK == 0 and MOBA_HEADS % heads_per_step == 0
    n_blocks = seq // MOBA_BLOCK
    n_tiles = seq // tq
    gw = heads_per_step * hd
    kern = functools.partial(_moba_prompt_kernel, n_blocks=n_blocks, n_heads=heads_per_step,
                             hd=hd, scale=hd ** -0.5, tq=tq)
    return pl.pallas_call(
        kern,
        grid=(batch, MOBA_HEADS // heads_per_step, n_tiles),
        in_specs=[
            pl.BlockSpec((tq, gw), lambda b, h, i: (b * n_tiles + i, h)),
            pl.BlockSpec((seq, gw), lambda b, h, i: (b, h)),
            pl.BlockSpec((seq, gw), lambda b, h, i: (b, h)),
        ],
        out_specs=pl.BlockSpec((tq, gw), lambda b, h, i: (b * n_tiles + i, h)),
        out_shape=jax.ShapeDtypeStruct((m, width), BF16),
        scratch_shapes=[pltpu.VMEM((heads_per_step, n_blocks, MOBA_BLOCK, hd), BF16),
                        pltpu.VMEM((heads_per_step, n_blocks, hd, MOBA_BLOCK), BF16),
                        pltpu.VMEM((heads_per_step, n_blocks, hd), F32)],
        compiler_params=_params("parallel", "parallel", "arbitrary"),
        name="moba_prompt",
    )(q, k, v)


def _decode_select_kernel(q_ref, kmean_ref, o_ref):
    gate = jnp.sum(kmean_ref[...] * q_ref[...], axis=-1, keepdims=True)
    _, picks = _top_blocks(gate, jnp.full(gate.shape, True), axis=0)
    for r, first in enumerate(picks):
        o_ref[r] = first[0]


def decode_select(q, kmean):
    batch, n_past, heads, hd = kmean.shape
    assert n_past >= MOBA_TOP_K
    return pl.pallas_call(
        _decode_select_kernel,
        grid=(batch,),
        in_specs=[pl.BlockSpec((None, heads, hd), lambda b: (b, 0, 0)),
                  pl.BlockSpec((None, n_past, heads, hd), lambda b: (b, 0, 0, 0))],
        out_specs=pl.BlockSpec((None, MOBA_TOP_K, heads, 1), lambda b: (b, 0, 0, 0)),
        out_shape=jax.ShapeDtypeStruct((batch, MOBA_TOP_K, heads, 1), jnp.int32),
        compiler_params=_params("parallel"),
        name="moba_decode_select",
    )(q, kmean)


def _decode_attend_kernel(pt_ref, top_ref, q_ref, kn_ref, vn_ref, ck_ref, cv_ref, o_ref,
                          kbuf_ref, vbuf_ref, sem_ref, *, heads, pages_per_block, n_pages_seq,
                          scale):
    b = pl.program_id(0)
    n_slots = MOBA_TOP_K * pages_per_block

    def head_copies(h):
        copies = []
        for r in range(MOBA_TOP_K):
            block = top_ref[(b * MOBA_TOP_K + r) * heads + h]
            for pg in range(pages_per_block):
                page = pt_ref[b * n_pages_seq + block * pages_per_block + pg]
                slot = r * pages_per_block + pg
                copies.append(pltpu.make_async_copy(
                    ck_ref.at[page, :, h, :], kbuf_ref.at[h, slot], sem_ref.at[0, h]))
                copies.append(pltpu.make_async_copy(
                    cv_ref.at[page, :, h, :], vbuf_ref.at[h, slot], sem_ref.at[1, h]))
        return copies

    all_copies = [head_copies(h) for h in range(heads)]
    for copies in all_copies:
        for cp in copies:
            cp.start()

    for h in range(heads):
        for cp in all_copies[h]:
            cp.wait()
        q = q_ref[h:h + 1, :]
        k_new, v_new = kn_ref[h:h + 1, :], vn_ref[h:h + 1, :]
        s_own = jnp.sum(q * k_new, axis=-1, keepdims=True) * scale
        s_past = [jnp.sum(kbuf_ref[h, i] * q, axis=-1, keepdims=True) * scale
                  for i in range(n_slots)]
        m = s_own
        for s in s_past:
            m = jnp.maximum(m, jnp.max(s, axis=0, keepdims=True))
        p_own = jnp.exp(s_own - m)
        denom = p_own
        acc = p_own * v_new
        for i, s in enumerate(s_past):
            p = jnp.exp(s - m)
            denom = denom + jnp.sum(p, axis=0, keepdims=True)
            acc = acc + jnp.sum(p * vbuf_ref[h, i], axis=0, keepdims=True)
        o_ref[h:h + 1, :] = acc / denom


def decode_attend(q, k_new, v_new, cache_k, cache_v, page_table_flat, top_flat, *, n_pages_seq):
    batch, heads, hd = q.shape
    page = cache_k.shape[1]
    ppb = MOBA_BLOCK // page
    n_slots = MOBA_TOP_K * ppb
    tok_spec = pl.BlockSpec((None, heads, hd), lambda b, pt, tp: (b, 0, 0))
    hbm_spec = pl.BlockSpec(memory_space=pl.ANY)
    return pl.pallas_call(
        functools.partial(_decode_attend_kernel, heads=heads, pages_per_block=ppb,
                          n_pages_seq=n_pages_seq, scale=hd ** -0.5),
        grid_spec=pltpu.PrefetchScalarGridSpec(
            num_scalar_prefetch=2,
            grid=(batch,),
            in_specs=[tok_spec, tok_spec, tok_spec, hbm_spec, hbm_spec],
            out_specs=tok_spec,
            scratch_shapes=[pltpu.VMEM((heads, n_slots, page, hd), F32),
                            pltpu.VMEM((heads, n_slots, page, hd), F32),
                            pltpu.SemaphoreType.DMA((2, heads))],
        ),
        out_shape=jax.ShapeDtypeStruct((batch, heads, hd), F32),
        compiler_params=_params("arbitrary"),
        name="moba_decode_attend",
    )(page_table_flat, top_flat, q, k_new, v_new, cache_k, cache_v)


def _weights_needed_after(layer, depth, n_gla):
    if layer + 1 >= depth:
        return []
    nxt = layer + 1
    needed = [("up", nxt), ("down", nxt)]
    if nxt < n_gla:
        needed += [("in", nxt), ("out", nxt)]
    else:
        needed += [("q", nxt - n_gla), ("o", nxt - n_gla)]
    if layer == n_gla - 1:
        needed.append(("kv", 0))
    return needed


def _trunk(x, w, bf16_w, *, batch, seq, raw_w=None, gla_s0=None, cache=None, kmean=None,
           kmean_side=None):
    n_gla, depth = w["n_gla"], w["depth"]
    heads = MOBA_HEADS
    h = x
    width = x.shape[1]
    hd = width // heads
    states = []
    kmean_parts = []
    k_new = v_new = None
    for l in range(depth):
        if l < n_gla:
            proj, = norm_matmul(h, w["norm_mix"][l], bf16_w["in", l], 0, n_cols=w["n_qkvr"],
                                tn=1536)
            b = gla_gate(h, w["norm_mix"][l], w["w_z"], w["w_gate"], w["b_gate"], l,
                         chunk=min(GLA_CHUNK, seq))
            if cache is None:
                og, s = gla_prompt(proj, b, w["head_norm"], l, batch=batch, seq=seq)
            else:
                og, s = gla_step(proj, b, w["head_norm"], gla_s0, l)
            states.append(s)
            h = matmul_residual(og, bf16_w["out", l], h, 0)
        else:
            i = l - n_gla
            q, = norm_matmul(h, w["norm_mix"][l], bf16_w["q", i], 0, tm=512, tn=2048)
            if cache is None:
                o = moba_prompt(q, k_new, v_new, batch=batch, seq=seq)
            else:
                cache_k, cache_v, page_table_flat, n_pages = cache
                q3 = q.reshape(batch, heads, hd)
                top = decode_select(q3, kmean)
                o = decode_attend(q3, k_new.reshape(batch, heads, hd),
                                  v_new.reshape(batch, heads, hd), cache_k, cache_v,
                                  page_table_flat, top.reshape(-1),
                                  n_pages_seq=n_pages).reshape(batch, width)
            h = matmul_residual(o, bf16_w["o", i], h, 0)
        todo = _weights_needed_after(l, depth, n_gla) if raw_w is not None else []
        side = None
        if kmean_side is not None:
            side = (*kmean_side, sum(part.shape[0] for part in kmean_parts))
        results = list(mlp_residual(
            h, w["norm_mlp"][l:l + 1], bf16_w["up", l], bf16_w["down", l], 0,
            final_gain=w["norm_final"] if l == depth - 1 else None,
            casts=[(raw_w[name], index) for name, index in todo], kmean_side=side))
        h = results.pop(0)
        if side is not None:
            kmean_parts.append(results.pop(0))
        for key, cast in zip(todo, results):
            bf16_w[key] = cast
        if l == n_gla - 1:
            k_new, v_new = norm_matmul(h, w["kv_norm"], bf16_w["kv", 0], 0, n_split=2, tn=512)
    return h, jnp.stack(states), k_new, v_new, kmean_parts


def kernel(x_prompt, x_sample, state_gla, cache_k, cache_v, page_table, norm_mix, norm_mlp,
           w_mlp_up, w_mlp_down, w_in_a, w_gate_a, b_gate_a, head_norm_a, w_out_a, kv_norm,
           w_kv, w_q_b, w_o_b, norm_final):
    batch, seq, d = x_prompt.shape
    dec_batch, dec_seq, _ = x_sample.shape
    assert dec_seq == 1
    n_qkvr = w_in_a.shape[2] - GLA_GATE_RANK
    w = dict(
        depth=w_mlp_up.shape[0], n_gla=w_in_a.shape[0], n_qkvr=n_qkvr,
        norm_mix=norm_mix, norm_mlp=norm_mlp, norm_final=norm_final, kv_norm=kv_norm,
        head_norm=head_norm_a, w_gate=w_gate_a, b_gate=b_gate_a, w_z=w_in_a[:, :, n_qkvr:],
    )
    raw_w = {"in": w_in_a, "out": w_out_a, "up": w_mlp_up, "down": w_mlp_down,
             "kv": w_kv[None], "q": w_q_b, "o": w_o_b}
    bf16_w = {(name, 0): raw_w[name][0:1].astype(BF16) for name in ("in", "out", "up", "down")}
    heads, hd = cache_k.shape[2], cache_k.shape[3]

    page_table_flat = page_table.reshape(-1)
    n_pages = page_table.shape[1]
    n_past = n_pages * cache_k.shape[1] // MOBA_BLOCK
    y_p, s_p, k_p, v_p, kmean_parts = _trunk(
        x_prompt.reshape(batch * seq, d), w, bf16_w, batch=batch, seq=seq, raw_w=raw_w,
        kmean_side=(cache_k, page_table_flat))
    kmean = jnp.concatenate(kmean_parts, axis=0)
    assert kmean.shape[0] == dec_batch * n_past, "prompt MLP grid steps must cover the cached blocks"
    cache = (cache_k, cache_v, page_table_flat, n_pages)
    y_s, s_s, k_s, v_s, _ = _trunk(
        x_sample.reshape(dec_batch, d), w, bf16_w, batch=dec_batch, seq=1, gla_s0=state_gla,
        cache=cache, kmean=kmean.reshape(dec_batch, n_past, heads, hd))
    return (y_p.reshape(batch, seq, d), y_s.reshape(dec_batch, 1, d), s_p, s_s,
            k_p.reshape(batch, seq, heads, hd), v_p.reshape(batch, seq, heads, hd),
            k_s.reshape(dec_batch, 1, heads, hd), v_s.reshape(dec_batch, 1, heads, hd))
```

```python
import functools

import jax
import jax.numpy as jnp
from jax import lax
from jax.experimental import pallas as pl
from jax.experimental.pallas import tpu as pltpu

F32 = jnp.float32
BF16 = jnp.bfloat16
HIGHEST = lax.Precision.HIGHEST

RMS_EPS = 1e-6
GLA_HEADS = 4
GLA_GATE_RANK = 16
GLA_GATE_TAU = 16.0
GLA_CHUNK = 64
GLA_SUB = 16
MOBA_HEADS = 16
MOBA_BLOCK = 256
MOBA_TOP_K = 3
LOG2_E = 1.4426950408889634
MASK_BIAS = 2.0 ** 100

V7X_VMEM_BYTES = 64 * 1024 * 1024
VMEM_LIMIT_BYTES = V7X_VMEM_BYTES - 8 * 1024 * 1024
BF16_SUBLANES = 16


def _params(*semantics):
    return pltpu.CompilerParams(dimension_semantics=semantics,
                                vmem_limit_bytes=VMEM_LIMIT_BYTES)


def _rms_normalize(x, gain):
    ms = jnp.mean(x * x, axis=-1, keepdims=True)
    return x * lax.rsqrt(ms + RMS_EPS) * gain


def _row_tile(m, want):
    return want if m % want == 0 else m


def _col_tile(n, want):
    while n % want:
        want //= 2
    return want


def _contract_last(a, b, **kw):
    return lax.dot_general(a, b, (((1,), (1,)), ((), ())), preferred_element_type=F32, **kw)


def _contract_first(a, b):
    return lax.dot_general(a, b, (((0,), (0,)), ((), ())), preferred_element_type=F32)


def _norm_matmul_kernel(x_ref, gain_ref, *refs, n_split):
    w_refs, o_refs, xn_ref = refs[:n_split], refs[n_split:2 * n_split], refs[2 * n_split]

    @pl.when(pl.program_id(1) == 0)
    def _():
        xn_ref[...] = _rms_normalize(x_ref[...], gain_ref[...]).astype(xn_ref.dtype)

    for w_ref, o_ref in zip(w_refs, o_refs):
        o_ref[...] = jnp.dot(xn_ref[...], w_ref[...], preferred_element_type=F32)


def norm_matmul(x, gain, w, layer, *, n_cols=None, n_split=1, tm=1024, tn=1024):
    m, d = x.shape
    n = w.shape[2] if n_cols is None else n_cols
    n_out = n // n_split
    tm = _row_tile(m, tm)
    tn = _col_tile(n_out, tn)
    per_split = n_out // tn

    def w_spec(s):
        return pl.BlockSpec((None, d, tn), lambda i, j: (layer, 0, s * per_split + j))

    return pl.pallas_call(
        functools.partial(_norm_matmul_kernel, n_split=n_split),
        grid=(m // tm, per_split),
        in_specs=[pl.BlockSpec((tm, d), lambda i, j: (i, 0)),
                  pl.BlockSpec((1, d), lambda i, j: (0, 0))]
                 + [w_spec(s) for s in range(n_split)],
        out_specs=[pl.BlockSpec((tm, tn), lambda i, j: (i, j))] * n_split,
        out_shape=[jax.ShapeDtypeStruct((m, n_out), F32)] * n_split,
        scratch_shapes=[pltpu.VMEM((tm, d), BF16)],
        compiler_params=_params("parallel", "arbitrary"),
        name="norm_matmul",
    )(x, gain.reshape(1, d), *([w] * n_split))


def _split_bf16(a):
    hi = a.astype(BF16)
    return hi, a - hi.astype(F32)


def _dot_split(a, b):
    a_hi, a_rest = _split_bf16(a)
    b_hi, b_rest = _split_bf16(b)
    a_lo, b_lo = a_rest.astype(BF16), b_rest.astype(BF16)
    return (jnp.dot(a_hi, b_hi, preferred_element_type=F32)
            + jnp.dot(a_lo, b_hi, preferred_element_type=F32)
            + jnp.dot(a_hi, b_lo, preferred_element_type=F32))


def _gla_gate_kernel(x_ref, gain_ref, wz_ref, wg_ref, bg_ref, o_ref, *, chunk):
    xn = _rms_normalize(x_ref[...], gain_ref[...])
    z = jnp.dot(xn.astype(BF16), wz_ref[...].astype(BF16), preferred_element_type=F32)
    logit = _dot_split(z, wg_ref[...]) + bg_ref[...]
    log_sig = jnp.minimum(logit, 0.0) - jnp.log(1.0 + jnp.exp(-jnp.abs(logit)))
    g = log_sig / GLA_GATE_TAU
    if chunk > 1:
        rows = g.shape[0]
        row = lax.broadcasted_iota(jnp.int32, (rows, rows), 0)
        col = lax.broadcasted_iota(jnp.int32, (rows, rows), 1)
        same_chunk_past = (row >= col) & (row // chunk == col // chunk)
        ones = jnp.where(same_chunk_past, 1.0, 0.0).astype(BF16)
        g_hi, g_rest = _split_bf16(g)
        g_mid, g_rest = _split_bf16(g_rest)
        g = (jnp.dot(ones, g_hi, preferred_element_type=F32)
             + jnp.dot(ones, g_mid, preferred_element_type=F32)
             + jnp.dot(ones, g_rest.astype(BF16), preferred_element_type=F32))
    o_ref[...] = g


def gla_gate(x, gain, w_z, w_gate, b_gate, layer, *, chunk, tm=256):
    m, d = x.shape
    rank, n = w_gate.shape[1:]
    tm = _row_tile(m, tm)
    assert chunk == 1 or tm % chunk == 0
    return pl.pallas_call(
        functools.partial(_gla_gate_kernel, chunk=chunk),
        grid=(m // tm,),
        in_specs=[
            pl.BlockSpec((tm, d), lambda i: (i, 0)),
            pl.BlockSpec((1, d), lambda i: (0, 0)),
            pl.BlockSpec((None, d, rank), lambda i: (layer, 0, 0)),
            pl.BlockSpec((None, rank, n), lambda i: (layer, 0, 0)),
            pl.BlockSpec((None, 1, n), lambda i: (layer, 0, 0)),
        ],
        out_specs=pl.BlockSpec((tm, n), lambda i: (i, 0)),
        out_shape=jax.ShapeDtypeStruct((m, n), F32),
        compiler_params=_params("parallel"),
        name="gla_gate",
    )(x, gain.reshape(1, d), w_z, w_gate, b_gate.reshape(b_gate.shape[0], 1, n))


def _gla_chunk(q, k, v, b, st):
    c, dk = q.shape
    n_sub = c // GLA_SUB
    row = lax.broadcasted_iota(jnp.int32, (c, c), 0)
    col = lax.broadcasted_iota(jnp.int32, (c, c), 1)

    def rows_of(r, n):
        return jnp.concatenate(
            [jnp.broadcast_to(b[j * GLA_SUB + r:j * GLA_SUB + r + 1, :], (GLA_SUB, dk))
             for j in range(n)], axis=0)

    b_end = b[c - 1:c, :]
    v_b = v.astype(BF16)

    k_end = (k * jnp.exp(b_end - b)).astype(BF16)
    st_new = st * jnp.exp(b_end) + _contract_first(v_b, k_end)

    from_start = b - rows_of(0, n_sub)
    q_diag = q * jnp.exp(from_start)
    k_diag = k * jnp.exp(-from_start)
    att_diag = _contract_last(q_diag.astype(BF16), k_diag.astype(BF16))
    same_sub = (row // GLA_SUB) == (col // GLA_SUB)
    att = jnp.where(same_sub & (row >= col), att_diag, 0.0)

    if n_sub > 1:
        past = c - GLA_SUB
        k_off = (k[:past] * jnp.exp(rows_of(GLA_SUB - 1, n_sub - 1) - b[:past])).astype(BF16)
        q_parts, k_parts = [], []
        for j in range(n_sub - 1):
            lo = (j + 1) * GLA_SUB
            q_j = (q[lo:] * jnp.exp(b[lo:] - b[lo - 1:lo, :])).astype(BF16)
            q_parts.append(jnp.concatenate([jnp.zeros((lo, dk), BF16), q_j], axis=0))
            pieces = [k_off[lo - GLA_SUB:lo], jnp.zeros((c - lo, dk), BF16)]
            if j:
                pieces.insert(0, jnp.zeros((lo - GLA_SUB, dk), BF16))
            k_parts.append(jnp.concatenate(pieces, axis=0))
        att = att + _contract_last(jnp.concatenate(q_parts, axis=1),
                                   jnp.concatenate(k_parts, axis=1))

    o = _contract_last((q * jnp.exp(b)).astype(BF16), st.astype(BF16))
    o = o + jnp.dot(att.astype(BF16), v_b, preferred_element_type=F32)
    return o, st_new


def _gla_prompt_kernel(q_ref, k_ref, v_ref, r_ref, b_ref, hn_ref, *refs, n_casts, n_chunks,
                       q_scale, unroll):
    cast_src, (og_ref, s_ref) = refs[:n_casts], refs[n_casts:n_casts + 2]
    cast_dst, st_ref = refs[n_casts + 2:2 * n_casts + 2], refs[2 * n_casts + 2]
    t = pl.program_id(2)
    for src, dst in zip(cast_src, cast_dst):
        dst[...] = src[...].astype(dst.dtype)

    @pl.when(t == 0)
    def _():
        st_ref[...] = jnp.zeros_like(st_ref)

    def chunk(ci, carry):
        rows = pl.ds(pl.multiple_of(ci * GLA_CHUNK, GLA_CHUNK), GLA_CHUNK)
        o, st_new = _gla_chunk(q_ref[rows, :] * q_scale, k_ref[rows, :], v_ref[rows, :],
                               b_ref[rows, :], st_ref[...])
        st_ref[...] = st_new
        r = r_ref[rows, :]
        silu = r / (1.0 + jnp.exp(-r))
        og_ref[rows, :] = (_rms_normalize(o, hn_ref[...]) * silu).astype(og_ref.dtype)
        return carry

    lax.fori_loop(0, n_chunks, chunk, 0, unroll=unroll)

    @pl.when(t == pl.num_programs(2) - 1)
    def _():
        s_ref[...] = st_ref[...].T


def _cast_side_stream(casts, grid):
    n_steps = 1
    for extent in grid:
        n_steps *= extent

    def step_of(*ids):
        step = 0
        for idx, extent in zip(ids, grid):
            step = step * extent + idx
        return step

    operands, in_specs, out_specs, out_shapes = [], [], [], []
    for src, src_layer in casts:
        _, rows, cols = src.shape
        slab = rows // n_steps
        assert rows == slab * n_steps and slab % BF16_SUBLANES == 0
        operands.append(src)
        in_specs.append(pl.BlockSpec(
            (None, slab, cols),
            lambda *ids, src_layer=src_layer: (src_layer, step_of(*ids[:len(grid)]), 0)))
        out_specs.append(pl.BlockSpec((None, slab, cols),
                                      lambda *ids: (0, step_of(*ids[:len(grid)]), 0)))
        out_shapes.append(jax.ShapeDtypeStruct((1, rows, cols), BF16))
    return operands, in_specs, out_specs, out_shapes


def gla_prompt(proj, b, head_norm, layer, *, batch, seq, tt=512, unroll=8, casts=()):
    m = batch * seq
    qk = b.shape[1]
    dk = qk // GLA_HEADS
    vw = (proj.shape[1] - 2 * qk) // 2
    dv = vw // GLA_HEADS
    tt = min(tt, seq)
    assert seq % tt == 0 and tt % GLA_CHUNK == 0 and dv % dk == 0
    nt = seq // tt
    k_col0 = qk // dk
    v_col0 = 2 * qk // dv
    r_col0 = (2 * qk + vw) // dv
    grid = (batch, GLA_HEADS, nt)
    cast_operands, cast_in, cast_out, cast_shapes = _cast_side_stream(casts, grid)
    kern = functools.partial(_gla_prompt_kernel, n_casts=len(casts),
                             n_chunks=tt // GLA_CHUNK, q_scale=dk ** -0.5, unroll=unroll)
    return pl.pallas_call(
        kern,
        grid=grid,
        in_specs=[
            pl.BlockSpec((tt, dk), lambda bi, h, t: (bi * nt + t, h)),
            pl.BlockSpec((tt, dk), lambda bi, h, t: (bi * nt + t, k_col0 + h)),
            pl.BlockSpec((tt, dv), lambda bi, h, t: (bi * nt + t, v_col0 + h)),
            pl.BlockSpec((tt, dv), lambda bi, h, t: (bi * nt + t, r_col0 + h)),
            pl.BlockSpec((tt, dk), lambda bi, h, t: (bi * nt + t, h)),
            pl.BlockSpec((None, 1, dv), lambda bi, h, t: (layer, 0, 0)),
        ] + cast_in,
        out_specs=[
            pl.BlockSpec((tt, dv), lambda bi, h, t: (bi * nt + t, h)),
            pl.BlockSpec((None, None, dk, dv), lambda bi, h, t: (bi, h, 0, 0)),
        ] + cast_out,
        out_shape=[
            jax.ShapeDtypeStruct((m, vw), BF16),
            jax.ShapeDtypeStruct((batch, GLA_HEADS, dk, dv), F32),
        ] + cast_shapes,
        scratch_shapes=[pltpu.VMEM((dv, dk), F32)],
        compiler_params=_params("parallel", "parallel", "arbitrary"),
        name="gla_prompt",
    )(proj, proj, proj, proj, b, head_norm.reshape(head_norm.shape[0], 1, dv), *cast_operands)


def _gla_step_kernel(q_ref, k_ref, g_ref, v_ref, r_ref, hn_ref, s0_ref, og_ref, s_ref,
                     *, q_scale):
    s_new = jnp.exp(g_ref[...]) * s0_ref[...] + k_ref[...] * v_ref[...]
    s_ref[...] = s_new
    o = jnp.sum((q_ref[...] * q_scale) * s_new, axis=0, keepdims=True)
    r = r_ref[...]
    silu = r / (1.0 + jnp.exp(-r))
    og_ref[...] = _rms_normalize(o, hn_ref[...]) * silu


def gla_step(proj, g, head_norm, s0, layer):
    batch = proj.shape[0]
    qk = g.shape[1]
    dk = qk // GLA_HEADS
    vw = (proj.shape[1] - 2 * qk) // 2
    dv = vw // GLA_HEADS
    col = lambda a: a.reshape(batch, GLA_HEADS, dk, 1)
    row = lambda a: a.reshape(batch, GLA_HEADS, 1, dv)
    q_c = col(proj[:, :qk])
    k_c = col(proj[:, qk:2 * qk])
    g_c = col(g)
    v_r = row(proj[:, 2 * qk:2 * qk + vw])
    r_r = row(proj[:, 2 * qk + vw:])
    col_spec = pl.BlockSpec((None, None, dk, 1), lambda bi, h: (bi, h, 0, 0))
    row_spec = pl.BlockSpec((None, None, 1, dv), lambda bi, h: (bi, h, 0, 0))
    st_spec = pl.BlockSpec((None, None, dk, dv), lambda bi, h: (bi, h, 0, 0))
    og, s = pl.pallas_call(
        functools.partial(_gla_step_kernel, q_scale=dk ** -0.5),
        grid=(batch, GLA_HEADS),
        in_specs=[col_spec, col_spec, col_spec, row_spec, row_spec,
                  pl.BlockSpec((None, 1, dv), lambda bi, h: (layer, 0, 0)),
                  pl.BlockSpec((None, None, None, dk, dv), lambda bi, h: (layer, bi, h, 0, 0))],
        out_specs=[row_spec, st_spec],
        out_shape=[jax.ShapeDtypeStruct((batch, GLA_HEADS, 1, dv), F32),
                   jax.ShapeDtypeStruct(s0.shape[1:], F32)],
        compiler_params=_params("parallel", "parallel"),
        name="gla_step",
    )(q_c, k_c, g_c, v_r, r_r, head_norm.reshape(head_norm.shape[0], 1, dv), s0)
    return og.reshape(batch, vw), s


def _matmul_residual_kernel(a_ref, w_ref, h_ref, o_ref):
    o_ref[...] = h_ref[...] + jnp.dot(a_ref[...].astype(BF16), w_ref[...],
                                      preferred_element_type=F32)


def matmul_residual(a, w, h, layer, *, tm=512, tn=2048):
    m, kd = a.shape
    n = w.shape[2]
    tm = _row_tile(m, tm)
    tn = _col_tile(n, tn)
    return pl.pallas_call(
        _matmul_residual_kernel,
        grid=(m // tm, n // tn),
        in_specs=[
            pl.BlockSpec((tm, kd), lambda i, j: (i, 0)),
            pl.BlockSpec((None, kd, tn), lambda i, j: (layer, 0, j)),
            pl.BlockSpec((tm, tn), lambda i, j: (i, j)),
        ],
        out_specs=pl.BlockSpec((tm, tn), lambda i, j: (i, j)),
        out_shape=jax.ShapeDtypeStruct((m, n), F32),
        compiler_params=_params("parallel", "parallel"),
        name="matmul_residual",
    )(a, w, h)


def _mlp_kernel(*refs, has_final_gain, n_casts, kmean):
    refs = list(refs)
    pt_ref = refs.pop(0) if kmean else None
    h_ref, gain_ref, wu_ref, wd_ref = refs[:4]
    del refs[:4]
    final_gain_ref = refs.pop(0) if has_final_gain else None
    cast_src = [refs.pop(0) for _ in range(n_casts)]
    ck_ref = refs.pop(0) if kmean else None
    o_ref = refs.pop(0)
    km_ref = refs.pop(0) if kmean else None
    cast_dst = [refs.pop(0) for _ in range(n_casts)]
    xn_ref = refs.pop(0)

    f, n_f = pl.program_id(1), pl.num_programs(1)
    if kmean:
        first_block, pages_per_block, block_len = kmean
        pages_ref, sem_ref = refs
        step = pl.program_id(0) * n_f + f
        slot = step % 2

        def page_copies(s, sl):
            first_page = (first_block + s) * pages_per_block
            return [pltpu.make_async_copy(ck_ref.at[pt_ref[first_page + pg]],
                                          pages_ref.at[sl, pg], sem_ref.at[sl])
                    for pg in range(pages_per_block)]

        @pl.when(step == 0)
        def _():
            for cp in page_copies(0, 0):
                cp.start()

        @pl.when(step + 1 < pl.num_programs(0) * n_f)
        def _():
            for cp in page_copies(step + 1, 1 - slot):
                cp.start()

        for cp in page_copies(step, slot):
            cp.wait()

    @pl.when(f == 0)
    def _():
        h = h_ref[...]
        xn_ref[...] = _rms_normalize(h, gain_ref[...]).astype(xn_ref.dtype)
        o_ref[...] = h

    u = jnp.maximum(jnp.dot(xn_ref[...], wu_ref[...], preferred_element_type=F32), 0.0)
    o_ref[...] += jnp.dot((u * u).astype(BF16), wd_ref[...], preferred_element_type=F32)

    if has_final_gain:
        @pl.when(f == n_f - 1)
        def _():
            o_ref[...] = _rms_normalize(o_ref[...], final_gain_ref[...])

    for src, dst in zip(cast_src, cast_dst):
        dst[...] = src[...].astype(dst.dtype)

    if kmean:
        total = jnp.sum(pages_ref[slot, 0], axis=0)
        for pg in range(1, pages_per_block):
            total = total + jnp.sum(pages_ref[slot, pg], axis=0)
        km_ref[...] = total / block_len


def mlp_residual(h, gain, w_up, w_down, layer, *, tm=512, tf=1024, final_gain=None,
                 casts=(), kmean_side=None):
    m, d = h.shape
    ff = w_up.shape[2]
    tm = _row_tile(m, tm)
    tf = _col_tile(ff, tf)
    grid = (m // tm, ff // tf)
    n_steps = grid[0] * grid[1]
    step_of = lambda i, f: i * grid[1] + f

    operands = [h, gain.reshape(gain.shape[0], 1, d), w_up, w_down]
    in_specs = [
        pl.BlockSpec((tm, d), lambda i, f, *_: (i, 0)),
        pl.BlockSpec((None, 1, d), lambda i, f, *_: (layer, 0, 0)),
        pl.BlockSpec((None, d, tf), lambda i, f, *_: (layer, 0, f)),
        pl.BlockSpec((None, tf, d), lambda i, f, *_: (layer, f, 0)),
    ]
    out_specs = [pl.BlockSpec((tm, d), lambda i, f, *_: (i, 0))]
    out_shapes = [jax.ShapeDtypeStruct((m, d), F32)]
    scratch = [pltpu.VMEM((tm, d), BF16)]
    if final_gain is not None:
        operands.append(final_gain.reshape(1, d))
        in_specs.append(pl.BlockSpec((1, d), lambda i, f, *_: (0, 0)))
    cast_operands, cast_in, cast_out_specs, cast_out_shapes = _cast_side_stream(casts, grid)
    operands += cast_operands
    in_specs += cast_in
    kmean = None
    if kmean_side is not None:
        cache_k, page_table_flat, first_block = kmean_side
        _, page, heads, hd = cache_k.shape
        ppb = MOBA_BLOCK // page
        kmean = (first_block, ppb, MOBA_BLOCK)
        operands = [page_table_flat] + operands + [cache_k]
        in_specs.append(pl.BlockSpec(memory_space=pl.ANY))
        out_specs.append(pl.BlockSpec((None, heads, hd), lambda i, f, *_: (step_of(i, f), 0, 0)))
        out_shapes.append(jax.ShapeDtypeStruct((n_steps, heads, hd), F32))
        scratch += [pltpu.VMEM((2, ppb, page, heads, hd), F32), pltpu.SemaphoreType.DMA((2,))]
    kern = functools.partial(_mlp_kernel, has_final_gain=final_gain is not None,
                             n_casts=len(casts), kmean=kmean)
    return pl.pallas_call(
        kern,
        grid_spec=pltpu.PrefetchScalarGridSpec(
            num_scalar_prefetch=1 if kmean else 0, grid=grid, in_specs=in_specs,
            out_specs=out_specs + cast_out_specs, scratch_shapes=scratch),
        out_shape=out_shapes + cast_out_shapes,
        compiler_params=_params("arbitrary" if kmean else "parallel", "arbitrary"),
        name="mlp_residual",
    )(*operands)


def _top_blocks(gate, is_past, axis):
    n_blocks = gate.shape[axis]
    idx = lax.broadcasted_iota(jnp.int32, gate.shape, axis)
    neg_inf = jnp.float32(-jnp.inf)
    left = jnp.where(is_past, gate, neg_inf)
    chosen = jnp.zeros(gate.shape, dtype=jnp.bool_)
    picks = []
    for _ in range(MOBA_TOP_K):
        best = jnp.max(left, axis=axis, keepdims=True)
        is_best = (left == best) & (best > neg_inf)
        first = jnp.min(jnp.where(is_best, idx, n_blocks), axis=axis, keepdims=True)
        pick = idx == first
        chosen = chosen | pick
        left = jnp.where(pick, neg_inf, left)
        picks.append(first)
    return chosen, picks


def _moba_prompt_kernel(q_ref, k_ref, v_ref, o_ref, kb_ref, vt_ref, kmean_ref,
                        *, n_blocks, n_heads, hd, scale, tq):
    ti = pl.program_id(2)
    blk = MOBA_BLOCK
    heads = range(n_heads)
    own_blocks = tq // blk

    @pl.when(ti == 0)
    def _():
        feature = lax.broadcasted_iota(jnp.int32, (blk, hd), 1)
        for j in heads:
            cols = slice(j * hd, (j + 1) * hd)
            for n in range(n_blocks):
                k_blk = k_ref[n * blk:(n + 1) * blk, cols]
                one_hot = jnp.where(feature == n, 1.0, 0.0).astype(BF16)
                kb_ref[j, n] = jnp.concatenate([k_blk.astype(BF16), one_hot], axis=1)
                vt_ref[j, n] = v_ref[n * blk:(n + 1) * blk, cols].T.astype(BF16)
                kmean_ref[j, n:n + 1, :] = jnp.mean(k_blk, axis=0, keepdims=True)

    q_pos = ti * tq + lax.broadcasted_iota(jnp.int32, (1, tq), 1)
    q_block = q_pos // blk
    block_id = lax.broadcasted_iota(jnp.int32, (n_blocks, tq), 0)
    key_off = lax.broadcasted_iota(jnp.int32, (blk, tq), 0)
    neg_inf = jnp.float32(-jnp.inf)

    q_bs, carry0 = [], []
    for j in heads:
        q = q_ref[:, j * hd:(j + 1) * hd]
        gate = _contract_last(kmean_ref[j], q, precision=HIGHEST)
        chosen, _ = _top_blocks(gate, block_id < q_block, axis=0)
        bias = jnp.where(chosen | (block_id == q_block), 0.0, -MASK_BIAS)
        bias = jnp.concatenate([bias, jnp.zeros((hd - n_blocks, tq), F32)], axis=0)
        q_bs.append(jnp.concatenate([(q * (scale * LOG2_E)).astype(BF16),
                                     bias.T.astype(BF16)], axis=1))
        carry0.append((jnp.full((1, tq), neg_inf), jnp.zeros((1, tq), F32),
                       jnp.zeros((hd, tq), F32)))

    def key_block(n, carry, causal):
        raw = [_contract_last(kb_ref[j, n], q_bs[j]) for j in heads]
        partial = []
        for j in heads:
            m_run, l_run, _ = carry[j]
            s = raw[j]
            if causal:
                s = jnp.where(key_off <= q_pos - n * blk, s, neg_inf)
            m_new = jnp.maximum(m_run, jnp.max(s, axis=0, keepdims=True))
            alpha = jnp.exp2(m_run - m_new)
            p = jnp.exp2(s - m_new)
            l_new = alpha * l_run + jnp.sum(p, axis=0, keepdims=True)
            partial.append((m_new, l_new, alpha, p.astype(BF16)))
        out = []
        for j in heads:
            m_new, l_new, alpha, p_b = partial[j]
            acc = alpha * carry[j][2] + jnp.dot(vt_ref[j, n], p_b,
                                                preferred_element_type=F32)
            out.append((m_new, l_new, acc))
        return tuple(out)

    first_own = ti * own_blocks
    carry = lax.fori_loop(0, first_own, functools.partial(key_block, causal=False),
                          tuple(carry0))
    for i in range(own_blocks):
        carry = key_block(first_own + i, carry, causal=True)
    for j in heads:
        _, l_fin, acc = carry[j]
        o_ref[:, j * hd:(j + 1) * hd] = (acc / l_fin).T.astype(o_ref.dtype)


def moba_prompt(q, k, v, *, batch, seq, heads_per_step=4, tq=512):
    m, width = q.shape
    hd = width // MOBA_HEADS
    tq = min(tq, seq)
    assert seq % tq == 0 and tq % MOBA_BLOCK == 0 and MOBA_HEADS % heads_per_step == 0
    n_blocks = seq // MOBA_BLOCK
    n_tiles = seq // tq
    gw = heads_per_step * hd
    kern = functools.partial(_moba_prompt_kernel, n_blocks=n_blocks, n_heads=heads_per_step,
                             hd=hd, scale=hd ** -0.5, tq=tq)
    return pl.pallas_call(
        kern,
        grid=(batch, MOBA_HEADS // heads_per_step, n_tiles),
        in_specs=[
            pl.BlockSpec((tq, gw), lambda b, h, i: (b * n_tiles + i, h)),
            pl.BlockSpec((seq, gw), lambda b, h, i: (b, h)),
            pl.BlockSpec((seq, gw), lambda b, h, i: (b, h)),
        ],
        out_specs=pl.BlockSpec((tq, gw), lambda b, h, i: (b * n_tiles + i, h)),
        out_shape=jax.ShapeDtypeStruct((m, width), BF16),
        scratch_shapes=[pltpu.VMEM((heads_per_step, n_blocks, MOBA_BLOCK, 2 * hd), BF16),
                        pltpu.VMEM((heads_per_step, n_blocks, hd, MOBA_BLOCK), BF16),
                        pltpu.VMEM((heads_per_step, n_blocks, hd), F32)],
        compiler_params=_params("parallel", "parallel", "arbitrary"),
        name="moba_prompt",
    )(q, k, v)


def _decode_select_kernel(q_ref, kmean_ref, o_ref):
    gate = jnp.sum(kmean_ref[...] * q_ref[...], axis=-1, keepdims=True)
    _, picks = _top_blocks(gate, jnp.full(gate.shape, True), axis=0)
    for r, first in enumerate(picks):
        o_ref[r] = first[0]


def decode_select(q, kmean):
    batch, n_past, heads, hd = kmean.shape
    assert n_past >= MOBA_TOP_K
    return pl.pallas_call(
        _decode_select_kernel,
        grid=(batch,),
        in_specs=[pl.BlockSpec((None, heads, hd), lambda b: (b, 0, 0)),
                  pl.BlockSpec((None, n_past, heads, hd), lambda b: (b, 0, 0, 0))],
        out_specs=pl.BlockSpec((None, MOBA_TOP_K, heads, 1), lambda b: (b, 0, 0, 0)),
        out_shape=jax.ShapeDtypeStruct((batch, MOBA_TOP_K, heads, 1), jnp.int32),
        compiler_params=_params("parallel"),
        name="moba_decode_select",
    )(q, kmean)


def _decode_attend_kernel(pt_ref, top_ref, q_ref, kn_ref, vn_ref, ck_ref, cv_ref, o_ref,
                          kbuf_ref, vbuf_ref, sem_ref, *, heads, pages_per_block, n_pages_seq,
                          scale):
    b = pl.program_id(0)
    n_slots = MOBA_TOP_K * pages_per_block

    def head_copies(h):
        copies = []
        for r in range(MOBA_TOP_K):
            block = top_ref[(b * MOBA_TOP_K + r) * heads + h]
            for pg in range(pages_per_block):
                page = pt_ref[b * n_pages_seq + block * pages_per_block + pg]
                slot = r * pages_per_block + pg
                copies.append(pltpu.make_async_copy(
                    ck_ref.at[page, :, h, :], kbuf_ref.at[h, slot], sem_ref.at[0, h]))
                copies.append(pltpu.make_async_copy(
                    cv_ref.at[page, :, h, :], vbuf_ref.at[h, slot], sem_ref.at[1, h]))
        return copies

    all_copies = [head_copies(h) for h in range(heads)]
    for copies in all_copies:
        for cp in copies:
            cp.start()

    for h in range(heads):
        for cp in all_copies[h]:
            cp.wait()
        q = q_ref[h:h + 1, :]
        k_new, v_new = kn_ref[h:h + 1, :], vn_ref[h:h + 1, :]
        s_own = jnp.sum(q * k_new, axis=-1, keepdims=True) * scale
        s_past = [jnp.sum(kbuf_ref[h, i] * q, axis=-1, keepdims=True) * scale
                  for i in range(n_slots)]
        m = s_own
        for s in s_past:
            m = jnp.maximum(m, jnp.max(s, axis=0, keepdims=True))
        p_own = jnp.exp(s_own - m)
        denom = p_own
        acc = p_own * v_new
        for i, s in enumerate(s_past):
            p = jnp.exp(s - m)
            denom = denom + jnp.sum(p, axis=0, keepdims=True)
            acc = acc + jnp.sum(p * vbuf_ref[h, i], axis=0, keepdims=True)
        o_ref[h:h + 1, :] = acc / denom


def decode_attend(q, k_new, v_new, cache_k, cache_v, page_table_flat, top_flat, *, n_pages_seq):
    batch, heads, hd = q.shape
    page = cache_k.shape[1]
    ppb = MOBA_BLOCK // page
    n_slots = MOBA_TOP_K * ppb
    tok_spec = pl.BlockSpec((None, heads, hd), lambda b, pt, tp: (b, 0, 0))
    hbm_spec = pl.BlockSpec(memory_space=pl.ANY)
    return pl.pallas_call(
        functools.partial(_decode_attend_kernel, heads=heads, pages_per_block=ppb,
                          n_pages_seq=n_pages_seq, scale=hd ** -0.5),
        grid_spec=pltpu.PrefetchScalarGridSpec(
            num_scalar_prefetch=2,
            grid=(batch,),
            in_specs=[tok_spec, tok_spec, tok_spec, hbm_spec, hbm_spec],
            out_specs=tok_spec,
            scratch_shapes=[pltpu.VMEM((heads, n_slots, page, hd), F32),
                            pltpu.VMEM((heads, n_slots, page, hd), F32),
                            pltpu.SemaphoreType.DMA((2, heads))],
        ),
        out_shape=jax.ShapeDtypeStruct((batch, heads, hd), F32),
        compiler_params=_params("arbitrary"),
        name="moba_decode_attend",
    )(page_table_flat, top_flat, q, k_new, v_new, cache_k, cache_v)


def _weights_needed_after(layer, depth, n_gla):
    if layer + 1 >= depth:
        return []
    nxt = layer + 1
    needed = [("up", nxt), ("down", nxt)]
    if nxt < n_gla:
        needed += [("in", nxt), ("out", nxt)]
    else:
        needed += [("q", nxt - n_gla), ("o", nxt - n_gla)]
    if layer == n_gla - 1:
        needed.append(("kv", 0))
    return needed


def _trunk(x, w, bf16_w, *, batch, seq, raw_w=None, gla_s0=None, cache=None, kmean=None,
           kmean_side=None):
    n_gla, depth = w["n_gla"], w["depth"]
    heads = MOBA_HEADS
    h = x
    width = x.shape[1]
    hd = width // heads
    states = []
    kmean_parts = []
    k_new = v_new = None
    for l in range(depth):
        if l < n_gla:
            proj, = norm_matmul(h, w["norm_mix"][l], bf16_w["in", l], 0, n_cols=w["n_qkvr"],
                                tn=1536)
            b = gla_gate(h, w["norm_mix"][l], w["w_z"], w["w_gate"], w["b_gate"], l,
                         chunk=min(GLA_CHUNK, seq))
            if cache is None:
                todo = [("up", l), ("down", l)] if raw_w is not None and l == 0 else []
                og, s, *cast = gla_prompt(proj, b, w["head_norm"], l, batch=batch, seq=seq,
                                          casts=[(raw_w[name], index) for name, index in todo])
                bf16_w.update(zip(todo, cast))
            else:
                og, s = gla_step(proj, b, w["head_norm"], gla_s0, l)
            states.append(s)
            h = matmul_residual(og, bf16_w["out", l], h, 0)
        else:
            i = l - n_gla
            q, = norm_matmul(h, w["norm_mix"][l], bf16_w["q", i], 0, tm=512, tn=2048)
            if cache is None:
                o = moba_prompt(q, k_new, v_new, batch=batch, seq=seq)
            else:
                cache_k, cache_v, page_table_flat, n_pages = cache
                q3 = q.reshape(batch, heads, hd)
                top = decode_select(q3, kmean)
                o = decode_attend(q3, k_new.reshape(batch, heads, hd),
                                  v_new.reshape(batch, heads, hd), cache_k, cache_v,
                                  page_table_flat, top.reshape(-1),
                                  n_pages_seq=n_pages).reshape(batch, width)
            h = matmul_residual(o, bf16_w["o", i], h, 0)
        todo = _weights_needed_after(l, depth, n_gla) if raw_w is not None else []
        side = None
        if kmean_side is not None:
            side = (*kmean_side, sum(part.shape[0] for part in kmean_parts))
        results = list(mlp_residual(
            h, w["norm_mlp"][l:l + 1], bf16_w["up", l], bf16_w["down", l], 0,
            final_gain=w["norm_final"] if l == depth - 1 else None,
            casts=[(raw_w[name], index) for name, index in todo], kmean_side=side))
        h = results.pop(0)
        if side is not None:
            kmean_parts.append(results.pop(0))
        for key, cast in zip(todo, results):
            bf16_w[key] = cast
        if l == n_gla - 1:
            k_new, v_new = norm_matmul(h, w["kv_norm"], bf16_w["kv", 0], 0, n_split=2, tn=512)
    return h, jnp.stack(states), k_new, v_new, kmean_parts


def kernel(x_prompt, x_sample, state_gla, cache_k, cache_v, page_table, norm_mix, norm_mlp,
           w_mlp_up, w_mlp_down, w_in_a, w_gate_a, b_gate_a, head_norm_a, w_out_a, kv_norm,
           w_kv, w_q_b, w_o_b, norm_final):
    batch, seq, d = x_prompt.shape
    dec_batch, dec_seq, _ = x_sample.shape
    assert dec_seq == 1
    n_qkvr = w_in_a.shape[2] - GLA_GATE_RANK
    w = dict(
        depth=w_mlp_up.shape[0], n_gla=w_in_a.shape[0], n_qkvr=n_qkvr,
        norm_mix=norm_mix, norm_mlp=norm_mlp, norm_final=norm_final, kv_norm=kv_norm,
        head_norm=head_norm_a, w_gate=w_gate_a, b_gate=b_gate_a, w_z=w_in_a[:, :, n_qkvr:],
    )
    raw_w = {"in": w_in_a, "out": w_out_a, "up": w_mlp_up, "down": w_mlp_down,
             "kv": w_kv[None], "q": w_q_b, "o": w_o_b}
    bf16_w = {(name, 0): raw_w[name][0:1].astype(BF16) for name in ("in", "out")}
    heads, hd = cache_k.shape[2], cache_k.shape[3]

    page_table_flat = page_table.reshape(-1)
    n_pages = page_table.shape[1]
    n_past = n_pages * cache_k.shape[1] // MOBA_BLOCK
    y_p, s_p, k_p, v_p, kmean_parts = _trunk(
        x_prompt.reshape(batch * seq, d), w, bf16_w, batch=batch, seq=seq, raw_w=raw_w,
        kmean_side=(cache_k, page_table_flat))
    kmean = jnp.concatenate(kmean_parts, axis=0)
    assert kmean.shape[0] == dec_batch * n_past, "prompt MLP grid steps must cover the cached blocks"
    cache = (cache_k, cache_v, page_table_flat, n_pages)
    y_s, s_s, k_s, v_s, _ = _trunk(
        x_sample.reshape(dec_batch, d), w, bf16_w, batch=dec_batch, seq=1, gla_s0=state_gla,
        cache=cache, kmean=kmean.reshape(dec_batch, n_past, heads, hd))
    return (y_p.reshape(batch, seq, d), y_s.reshape(dec_batch, 1, d), s_p, s_s,
            k_p.reshape(batch, seq, heads, hd), v_p.reshape(batch, seq, heads, hd),
            k_s.reshape(dec_batch, 1, heads, hd), v_s.reshape(dec_batch, 1, heads, hd))
```

```python
import functools

import jax
import jax.numpy as jnp
from jax import lax
from jax.experimental import pallas as pl
from jax.experimental.pallas import tpu as pltpu

F32 = jnp.float32
BF16 = jnp.bfloat16
HIGHEST = lax.Precision.HIGHEST

RMS_EPS = 1e-6
GLA_HEADS = 4
GLA_GATE_RANK = 16
GLA_GATE_TAU = 16.0
GLA_CHUNK = 64
GLA_SUB = 16
MOBA_HEADS = 16
MOBA_BLOCK = 256
MOBA_TOP_K = 3
LOG2_E = 1.4426950408889634
MASK_BIAS = 2.0 ** 100

V7X_VMEM_BYTES = 64 * 1024 * 1024
VMEM_LIMIT_BYTES = V7X_VMEM_BYTES - 8 * 1024 * 1024
BF16_SUBLANES = 16


def _params(*semantics):
    return pltpu.CompilerParams(dimension_semantics=semantics,
                                vmem_limit_bytes=VMEM_LIMIT_BYTES)


def _rms_normalize(x, gain):
    ms = jnp.mean(x * x, axis=-1, keepdims=True)
    return x * lax.rsqrt(ms + RMS_EPS) * gain


def _row_tile(m, want):
    return want if m % want == 0 else m


def _col_tile(n, want):
    while n % want:
        want //= 2
    return want


def _contract_last(a, b, **kw):
    return lax.dot_general(a, b, (((1,), (1,)), ((), ())), preferred_element_type=F32, **kw)


def _contract_first(a, b):
    return lax.dot_general(a, b, (((0,), (0,)), ((), ())), preferred_element_type=F32)


def _norm_matmul_kernel(x_ref, gain_ref, *refs, n_split):
    w_refs, o_refs, xn_ref = refs[:n_split], refs[n_split:2 * n_split], refs[2 * n_split]

    @pl.when(pl.program_id(1) == 0)
    def _():
        xn_ref[...] = _rms_normalize(x_ref[...], gain_ref[...]).astype(xn_ref.dtype)

    for w_ref, o_ref in zip(w_refs, o_refs):
        o_ref[...] = jnp.dot(xn_ref[...], w_ref[...], preferred_element_type=F32)


def norm_matmul(x, gain, w, layer, *, n_cols=None, n_split=1, tm=1024, tn=1024):
    m, d = x.shape
    n = w.shape[2] if n_cols is None else n_cols
    n_out = n // n_split
    tm = _row_tile(m, tm)
    tn = _col_tile(n_out, tn)
    per_split = n_out // tn

    def w_spec(s):
        return pl.BlockSpec((None, d, tn), lambda i, j: (layer, 0, s * per_split + j))

    return pl.pallas_call(
        functools.partial(_norm_matmul_kernel, n_split=n_split),
        grid=(m // tm, per_split),
        in_specs=[pl.BlockSpec((tm, d), lambda i, j: (i, 0)),
                  pl.BlockSpec((1, d), lambda i, j: (0, 0))]
                 + [w_spec(s) for s in range(n_split)],
        out_specs=[pl.BlockSpec((tm, tn), lambda i, j: (i, j))] * n_split,
        out_shape=[jax.ShapeDtypeStruct((m, n_out), F32)] * n_split,
        scratch_shapes=[pltpu.VMEM((tm, d), BF16)],
        compiler_params=_params("parallel", "arbitrary"),
        name="norm_matmul",
    )(x, gain.reshape(1, d), *([w] * n_split))


def _split_bf16(a):
    hi = a.astype(BF16)
    return hi, a - hi.astype(F32)


def _dot_split(a, b):
    a_hi, a_rest = _split_bf16(a)
    b_hi, b_rest = _split_bf16(b)
    a_lo, b_lo = a_rest.astype(BF16), b_rest.astype(BF16)
    return (jnp.dot(a_hi, b_hi, preferred_element_type=F32)
            + jnp.dot(a_lo, b_hi, preferred_element_type=F32)
            + jnp.dot(a_hi, b_lo, preferred_element_type=F32))


def _gla_gate_kernel(x_ref, gain_ref, wz_ref, wg_ref, bg_ref, o_ref, *, chunk):
    xn = _rms_normalize(x_ref[...], gain_ref[...])
    z = jnp.dot(xn.astype(BF16), wz_ref[...].astype(BF16), preferred_element_type=F32)
    logit = _dot_split(z, wg_ref[...]) + bg_ref[...]
    log_sig = jnp.minimum(logit, 0.0) - jnp.log(1.0 + jnp.exp(-jnp.abs(logit)))
    g = log_sig / GLA_GATE_TAU
    if chunk > 1:
        rows = g.shape[0]
        row = lax.broadcasted_iota(jnp.int32, (rows, rows), 0)
        col = lax.broadcasted_iota(jnp.int32, (rows, rows), 1)
        same_chunk_past = (row >= col) & (row // chunk == col // chunk)
        ones = jnp.where(same_chunk_past, 1.0, 0.0).astype(BF16)
        g_hi, g_rest = _split_bf16(g)
        g_mid, g_rest = _split_bf16(g_rest)
        g = (jnp.dot(ones, g_hi, preferred_element_type=F32)
             + jnp.dot(ones, g_mid, preferred_element_type=F32)
             + jnp.dot(ones, g_rest.astype(BF16), preferred_element_type=F32))
    o_ref[...] = g


def gla_gate(x, gain, w_z, w_gate, b_gate, layer, *, chunk, tm=256):
    m, d = x.shape
    rank, n = w_gate.shape[1:]
    tm = _row_tile(m, tm)
    assert chunk == 1 or tm % chunk == 0
    return pl.pallas_call(
        functools.partial(_gla_gate_kernel, chunk=chunk),
        grid=(m // tm,),
        in_specs=[
            pl.BlockSpec((tm, d), lambda i: (i, 0)),
            pl.BlockSpec((1, d), lambda i: (0, 0)),
            pl.BlockSpec((None, d, rank), lambda i: (layer, 0, 0)),
            pl.BlockSpec((None, rank, n), lambda i: (layer, 0, 0)),
            pl.BlockSpec((None, 1, n), lambda i: (layer, 0, 0)),
        ],
        out_specs=pl.BlockSpec((tm, n), lambda i: (i, 0)),
        out_shape=jax.ShapeDtypeStruct((m, n), F32),
        compiler_params=_params("parallel"),
        name="gla_gate",
    )(x, gain.reshape(1, d), w_z, w_gate, b_gate.reshape(b_gate.shape[0], 1, n))


def _gla_chunk(q, k, v, b, st):
    c, dk = q.shape
    n_sub = c // GLA_SUB
    row = lax.broadcasted_iota(jnp.int32, (c, c), 0)
    col = lax.broadcasted_iota(jnp.int32, (c, c), 1)

    def rows_of(r, n):
        return jnp.concatenate(
            [jnp.broadcast_to(b[j * GLA_SUB + r:j * GLA_SUB + r + 1, :], (GLA_SUB, dk))
             for j in range(n)], axis=0)

    b_end = b[c - 1:c, :]
    v_b = v.astype(BF16)

    k_end = (k * jnp.exp(b_end - b)).astype(BF16)
    st_new = st * jnp.exp(b_end) + _contract_first(v_b, k_end)

    from_start = b - rows_of(0, n_sub)
    q_diag = q * jnp.exp(from_start)
    k_diag = k * jnp.exp(-from_start)
    att_diag = _contract_last(q_diag.astype(BF16), k_diag.astype(BF16))
    same_sub = (row // GLA_SUB) == (col // GLA_SUB)
    att = jnp.where(same_sub & (row >= col), att_diag, 0.0)

    if n_sub > 1:
        past = c - GLA_SUB
        k_off = (k[:past] * jnp.exp(rows_of(GLA_SUB - 1, n_sub - 1) - b[:past])).astype(BF16)
        q_parts, k_parts = [], []
        for j in range(n_sub - 1):
            lo = (j + 1) * GLA_SUB
            q_j = (q[lo:] * jnp.exp(b[lo:] - b[lo - 1:lo, :])).astype(BF16)
            q_parts.append(jnp.concatenate([jnp.zeros((lo, dk), BF16), q_j], axis=0))
            pieces = [k_off[lo - GLA_SUB:lo], jnp.zeros((c - lo, dk), BF16)]
            if j:
                pieces.insert(0, jnp.zeros((lo - GLA_SUB, dk), BF16))
            k_parts.append(jnp.concatenate(pieces, axis=0))
        att = att + _contract_last(jnp.concatenate(q_parts, axis=1),
                                   jnp.concatenate(k_parts, axis=1))

    o = _contract_last((q * jnp.exp(b)).astype(BF16), st.astype(BF16))
    o = o + jnp.dot(att.astype(BF16), v_b, preferred_element_type=F32)
    return o, st_new


def _gla_prompt_kernel(q_ref, k_ref, v_ref, r_ref, b_ref, hn_ref, *refs, n_casts, n_chunks,
                       q_scale, unroll):
    cast_src, (og_ref, s_ref) = refs[:n_casts], refs[n_casts:n_casts + 2]
    cast_dst, st_ref = refs[n_casts + 2:2 * n_casts + 2], refs[2 * n_casts + 2]
    t = pl.program_id(2)
    for src, dst in zip(cast_src, cast_dst):
        dst[...] = src[...].astype(dst.dtype)

    @pl.when(t == 0)
    def _():
        st_ref[...] = jnp.zeros_like(st_ref)

    def chunk(ci, carry):
        rows = pl.ds(pl.multiple_of(ci * GLA_CHUNK, GLA_CHUNK), GLA_CHUNK)
        o, st_new = _gla_chunk(q_ref[rows, :] * q_scale, k_ref[rows, :], v_ref[rows, :],
                               b_ref[rows, :], st_ref[...])
        st_ref[...] = st_new
        r = r_ref[rows, :]
        silu = r / (1.0 + jnp.exp(-r))
        og_ref[rows, :] = (_rms_normalize(o, hn_ref[...]) * silu).astype(og_ref.dtype)
        return carry

    lax.fori_loop(0, n_chunks, chunk, 0, unroll=unroll)

    @pl.when(t == pl.num_programs(2) - 1)
    def _():
        s_ref[...] = st_ref[...].T


def _cast_side_stream(casts, grid):
    n_steps = 1
    for extent in grid:
        n_steps *= extent

    def step_of(*ids):
        step = 0
        for idx, extent in zip(ids, grid):
            step = step * extent + idx
        return step

    operands, in_specs, out_specs, out_shapes = [], [], [], []
    for src, src_layer in casts:
        _, rows, cols = src.shape
        slab = rows // n_steps
        assert rows == slab * n_steps and slab % BF16_SUBLANES == 0
        operands.append(src)
        in_specs.append(pl.BlockSpec(
            (None, slab, cols),
            lambda *ids, src_layer=src_layer: (src_layer, step_of(*ids[:len(grid)]), 0)))
        out_specs.append(pl.BlockSpec((None, slab, cols),
                                      lambda *ids: (0, step_of(*ids[:len(grid)]), 0)))
        out_shapes.append(jax.ShapeDtypeStruct((1, rows, cols), BF16))
    return operands, in_specs, out_specs, out_shapes


def gla_prompt(proj, b, head_norm, layer, *, batch, seq, tt=1024, unroll=8, casts=()):
    m = batch * seq
    qk = b.shape[1]
    dk = qk // GLA_HEADS
    vw = (proj.shape[1] - 2 * qk) // 2
    dv = vw // GLA_HEADS
    tt = min(tt, seq)
    assert seq % tt == 0 and tt % GLA_CHUNK == 0 and dv % dk == 0
    nt = seq // tt
    k_col0 = qk // dk
    v_col0 = 2 * qk // dv
    r_col0 = (2 * qk + vw) // dv
    grid = (batch, GLA_HEADS, nt)
    cast_operands, cast_in, cast_out, cast_shapes = _cast_side_stream(casts, grid)
    kern = functools.partial(_gla_prompt_kernel, n_casts=len(casts),
                             n_chunks=tt // GLA_CHUNK, q_scale=dk ** -0.5, unroll=unroll)
    return pl.pallas_call(
        kern,
        grid=grid,
        in_specs=[
            pl.BlockSpec((tt, dk), lambda bi, h, t: (bi * nt + t, h)),
            pl.BlockSpec((tt, dk), lambda bi, h, t: (bi * nt + t, k_col0 + h)),
            pl.BlockSpec((tt, dv), lambda bi, h, t: (bi * nt + t, v_col0 + h)),
            pl.BlockSpec((tt, dv), lambda bi, h, t: (bi * nt + t, r_col0 + h)),
            pl.BlockSpec((tt, dk), lambda bi, h, t: (bi * nt + t, h)),
            pl.BlockSpec((None, 1, dv), lambda bi, h, t: (layer, 0, 0)),
        ] + cast_in,
        out_specs=[
            pl.BlockSpec((tt, dv), lambda bi, h, t: (bi * nt + t, h)),
            pl.BlockSpec((None, None, dk, dv), lambda bi, h, t: (bi, h, 0, 0)),
        ] + cast_out,
        out_shape=[
            jax.ShapeDtypeStruct((m, vw), BF16),
            jax.ShapeDtypeStruct((batch, GLA_HEADS, dk, dv), F32),
        ] + cast_shapes,
        scratch_shapes=[pltpu.VMEM((dv, dk), F32)],
        compiler_params=_params("parallel", "parallel", "arbitrary"),
        name="gla_prompt",
    )(proj, proj, proj, proj, b, head_norm.reshape(head_norm.shape[0], 1, dv), *cast_operands)


def _gla_step_kernel(qkg_ref, v_ref, r_ref, hn_ref, s0_ref, og_ref, s_ref, *, q_scale):
    for h in range(GLA_HEADS):
        q, k, g = (qkg_ref[h, :, i:i + 1] for i in range(3))
        s_new = jnp.exp(g) * s0_ref[h] + k * v_ref[h]
        s_ref[h] = s_new
        o = jnp.sum((q * q_scale) * s_new, axis=0, keepdims=True)
        r = r_ref[h]
        silu = r / (1.0 + jnp.exp(-r))
        og_ref[h] = _rms_normalize(o, hn_ref[...]) * silu


def gla_step(proj, g, head_norm, s0, layer):
    batch = proj.shape[0]
    qk = g.shape[1]
    dk = qk // GLA_HEADS
    vw = (proj.shape[1] - 2 * qk) // 2
    dv = vw // GLA_HEADS
    qkg = jnp.stack([proj[:, :qk], proj[:, qk:2 * qk], g], axis=-1)
    qkg = qkg.reshape(batch, GLA_HEADS, dk, 3)
    row = lambda a: a.reshape(batch, GLA_HEADS, 1, dv)
    v_r = row(proj[:, 2 * qk:2 * qk + vw])
    r_r = row(proj[:, 2 * qk + vw:])
    row_spec = pl.BlockSpec((None, GLA_HEADS, 1, dv), lambda bi: (bi, 0, 0, 0))
    og, s = pl.pallas_call(
        functools.partial(_gla_step_kernel, q_scale=dk ** -0.5),
        grid=(batch,),
        in_specs=[pl.BlockSpec((None, GLA_HEADS, dk, 3), lambda bi: (bi, 0, 0, 0)),
                  row_spec, row_spec,
                  pl.BlockSpec((None, 1, dv), lambda bi: (layer, 0, 0)),
                  pl.BlockSpec((None, None, GLA_HEADS, dk, dv), lambda bi: (layer, bi, 0, 0, 0))],
        out_specs=[row_spec,
                   pl.BlockSpec((None, GLA_HEADS, dk, dv), lambda bi: (bi, 0, 0, 0))],
        out_shape=[jax.ShapeDtypeStruct((batch, GLA_HEADS, 1, dv), F32),
                   jax.ShapeDtypeStruct(s0.shape[1:], F32)],
        compiler_params=_params("parallel"),
        name="gla_step",
    )(qkg, v_r, r_r, head_norm.reshape(head_norm.shape[0], 1, dv), s0)
    return og.reshape(batch, vw), s


def _matmul_residual_kernel(a_ref, w_ref, h_ref, o_ref):
    o_ref[...] = h_ref[...] + jnp.dot(a_ref[...].astype(BF16), w_ref[...],
                                      preferred_element_type=F32)


def matmul_residual(a, w, h, layer, *, tm=512, tn=2048):
    m, kd = a.shape
    n = w.shape[2]
    tm = _row_tile(m, tm)
    tn = _col_tile(n, tn)
    return pl.pallas_call(
        _matmul_residual_kernel,
        grid=(m // tm, n // tn),
        in_specs=[
            pl.BlockSpec((tm, kd), lambda i, j: (i, 0)),
            pl.BlockSpec((None, kd, tn), lambda i, j: (layer, 0, j)),
            pl.BlockSpec((tm, tn), lambda i, j: (i, j)),
        ],
        out_specs=pl.BlockSpec((tm, tn), lambda i, j: (i, j)),
        out_shape=jax.ShapeDtypeStruct((m, n), F32),
        compiler_params=_params("parallel", "parallel"),
        name="matmul_residual",
    )(a, w, h)


def _mlp_kernel(*refs, has_final_gain, n_casts, kmean):
    refs = list(refs)
    pt_ref = refs.pop(0) if kmean else None
    h_ref, gain_ref, wu_ref, wd_ref = refs[:4]
    del refs[:4]
    final_gain_ref = refs.pop(0) if has_final_gain else None
    cast_src = [refs.pop(0) for _ in range(n_casts)]
    ck_ref = refs.pop(0) if kmean else None
    o_ref = refs.pop(0)
    km_ref = refs.pop(0) if kmean else None
    cast_dst = [refs.pop(0) for _ in range(n_casts)]
    xn_ref = refs.pop(0)

    f, n_f = pl.program_id(1), pl.num_programs(1)
    if kmean:
        first_block, pages_per_block, block_len = kmean
        pages_ref, sem_ref = refs
        step = pl.program_id(0) * n_f + f
        slot = step % 2

        def page_copies(s, sl):
            first_page = (first_block + s) * pages_per_block
            return [pltpu.make_async_copy(ck_ref.at[pt_ref[first_page + pg]],
                                          pages_ref.at[sl, pg], sem_ref.at[sl])
                    for pg in range(pages_per_block)]

        @pl.when(step == 0)
        def _():
            for cp in page_copies(0, 0):
                cp.start()

        @pl.when(step + 1 < pl.num_programs(0) * n_f)
        def _():
            for cp in page_copies(step + 1, 1 - slot):
                cp.start()

        for cp in page_copies(step, slot):
            cp.wait()

    @pl.when(f == 0)
    def _():
        h = h_ref[...]
        xn_ref[...] = _rms_normalize(h, gain_ref[...]).astype(xn_ref.dtype)
        o_ref[...] = h

    u = jnp.maximum(jnp.dot(xn_ref[...], wu_ref[...], preferred_element_type=F32), 0.0)
    o_ref[...] += jnp.dot((u * u).astype(BF16), wd_ref[...], preferred_element_type=F32)

    if has_final_gain:
        @pl.when(f == n_f - 1)
        def _():
            o_ref[...] = _rms_normalize(o_ref[...], final_gain_ref[...])

    for src, dst in zip(cast_src, cast_dst):
        dst[...] = src[...].astype(dst.dtype)

    if kmean:
        total = jnp.sum(pages_ref[slot, 0], axis=0)
        for pg in range(1, pages_per_block):
            total = total + jnp.sum(pages_ref[slot, pg], axis=0)
        km_ref[...] = total / block_len


def mlp_residual(h, gain, w_up, w_down, layer, *, tm=512, tf=1024, final_gain=None,
                 casts=(), kmean_side=None):
    m, d = h.shape
    ff = w_up.shape[2]
    tm = _row_tile(m, tm)
    tf = _col_tile(ff, tf)
    grid = (m // tm, ff // tf)
    n_steps = grid[0] * grid[1]
    step_of = lambda i, f: i * grid[1] + f

    operands = [h, gain.reshape(gain.shape[0], 1, d), w_up, w_down]
    in_specs = [
        pl.BlockSpec((tm, d), lambda i, f, *_: (i, 0)),
        pl.BlockSpec((None, 1, d), lambda i, f, *_: (layer, 0, 0)),
        pl.BlockSpec((None, d, tf), lambda i, f, *_: (layer, 0, f)),
        pl.BlockSpec((None, tf, d), lambda i, f, *_: (layer, f, 0)),
    ]
    out_specs = [pl.BlockSpec((tm, d), lambda i, f, *_: (i, 0))]
    out_shapes = [jax.ShapeDtypeStruct((m, d), F32)]
    scratch = [pltpu.VMEM((tm, d), BF16)]
    if final_gain is not None:
        operands.append(final_gain.reshape(1, d))
        in_specs.append(pl.BlockSpec((1, d), lambda i, f, *_: (0, 0)))
    cast_operands, cast_in, cast_out_specs, cast_out_shapes = _cast_side_stream(casts, grid)
    operands += cast_operands
    in_specs += cast_in
    kmean = None
    if kmean_side is not None:
        cache_k, page_table_flat, first_block = kmean_side
        _, page, heads, hd = cache_k.shape
        ppb = MOBA_BLOCK // page
        kmean = (first_block, ppb, MOBA_BLOCK)
        operands = [page_table_flat] + operands + [cache_k]
        in_specs.append(pl.BlockSpec(memory_space=pl.ANY))
        out_specs.append(pl.BlockSpec((None, heads, hd), lambda i, f, *_: (step_of(i, f), 0, 0)))
        out_shapes.append(jax.ShapeDtypeStruct((n_steps, heads, hd), F32))
        scratch += [pltpu.VMEM((2, ppb, page, heads, hd), F32), pltpu.SemaphoreType.DMA((2,))]
    kern = functools.partial(_mlp_kernel, has_final_gain=final_gain is not None,
                             n_casts=len(casts), kmean=kmean)
    return pl.pallas_call(
        kern,
        grid_spec=pltpu.PrefetchScalarGridSpec(
            num_scalar_prefetch=1 if kmean else 0, grid=grid, in_specs=in_specs,
            out_specs=out_specs + cast_out_specs, scratch_shapes=scratch),
        out_shape=out_shapes + cast_out_shapes,
        compiler_params=_params("arbitrary" if kmean else "parallel", "arbitrary"),
        name="mlp_residual",
    )(*operands)


def _top_blocks(gate, is_past, axis):
    n_blocks = gate.shape[axis]
    idx = lax.broadcasted_iota(jnp.int32, gate.shape, axis)
    neg_inf = jnp.float32(-jnp.inf)
    left = jnp.where(is_past, gate, neg_inf)
    chosen = jnp.zeros(gate.shape, dtype=jnp.bool_)
    picks = []
    for _ in range(MOBA_TOP_K):
        best = jnp.max(left, axis=axis, keepdims=True)
        is_best = (left == best) & (best > neg_inf)
        first = jnp.min(jnp.where(is_best, idx, n_blocks), axis=axis, keepdims=True)
        pick = idx == first
        chosen = chosen | pick
        left = jnp.where(pick, neg_inf, left)
        picks.append(first)
    return chosen, picks


def _moba_prompt_kernel(q_ref, k_ref, v_ref, o_ref, kb_ref, vt_ref, kmean_ref,
                        *, n_blocks, n_heads, hd, scale, tq):
    ti = pl.program_id(2)
    blk = MOBA_BLOCK
    heads = range(n_heads)
    own_blocks = tq // blk

    @pl.when(ti == 0)
    def _():
        feature = lax.broadcasted_iota(jnp.int32, (blk, hd), 1)
        for j in heads:
            cols = slice(j * hd, (j + 1) * hd)
            for n in range(n_blocks):
                k_blk = k_ref[n * blk:(n + 1) * blk, cols]
                one_hot = jnp.where(feature == n, 1.0, 0.0).astype(BF16)
                kb_ref[j, n] = jnp.concatenate([k_blk.astype(BF16), one_hot], axis=1)
                vt_ref[j, n] = v_ref[n * blk:(n + 1) * blk, cols].T.astype(BF16)
                kmean_ref[j, n:n + 1, :] = jnp.mean(k_blk, axis=0, keepdims=True)

    q_pos = ti * tq + lax.broadcasted_iota(jnp.int32, (1, tq), 1)
    q_block = q_pos // blk
    block_id = lax.broadcasted_iota(jnp.int32, (n_blocks, tq), 0)
    key_off = lax.broadcasted_iota(jnp.int32, (blk, tq), 0)
    neg_inf = jnp.float32(-jnp.inf)

    q_bs, carry0 = [], []
    for j in heads:
        q = q_ref[:, j * hd:(j + 1) * hd]
        gate = _contract_last(kmean_ref[j], q, precision=HIGHEST)
        chosen, _ = _top_blocks(gate, block_id < q_block, axis=0)
        bias = jnp.where(chosen | (block_id == q_block), 0.0, -MASK_BIAS)
        bias = jnp.concatenate([bias, jnp.zeros((hd - n_blocks, tq), F32)], axis=0)
        q_bs.append(jnp.concatenate([(q * (scale * LOG2_E)).astype(BF16),
                                     bias.T.astype(BF16)], axis=1))
        carry0.append((jnp.full((1, tq), neg_inf), jnp.zeros((1, tq), F32),
                       jnp.zeros((hd, tq), F32)))

    def key_block(n, carry, causal):
        raw = [_contract_last(kb_ref[j, n], q_bs[j]) for j in heads]
        partial = []
        for j in heads:
            m_run, l_run, _ = carry[j]
            s = raw[j]
            if causal:
                s = jnp.where(key_off <= q_pos - n * blk, s, neg_inf)
            m_new = jnp.maximum(m_run, jnp.max(s, axis=0, keepdims=True))
            alpha = jnp.exp2(m_run - m_new)
            p = jnp.exp2(s - m_new)
            l_new = alpha * l_run + jnp.sum(p, axis=0, keepdims=True)
            partial.append((m_new, l_new, alpha, p.astype(BF16)))
        out = []
        for j in heads:
            m_new, l_new, alpha, p_b = partial[j]
            acc = alpha * carry[j][2] + jnp.dot(vt_ref[j, n], p_b,
                                                preferred_element_type=F32)
            out.append((m_new, l_new, acc))
        return tuple(out)

    first_own = ti * own_blocks
    carry = lax.fori_loop(0, first_own, functools.partial(key_block, causal=False),
                          tuple(carry0))
    for i in range(own_blocks):
        carry = key_block(first_own + i, carry, causal=True)
    for j in heads:
        _, l_fin, acc = carry[j]
        o_ref[:, j * hd:(j + 1) * hd] = (acc / l_fin).T.astype(o_ref.dtype)


def moba_prompt(q, k, v, *, batch, seq, heads_per_step=4, tq=512):
    m, width = q.shape
    hd = width // MOBA_HEADS
    tq = min(tq, seq)
    assert seq % tq == 0 and tq % MOBA_BLOCK == 0 and MOBA_HEADS % heads_per_step == 0
    n_blocks = seq // MOBA_BLOCK
    n_tiles = seq // tq
    gw = heads_per_step * hd
    kern = functools.partial(_moba_prompt_kernel, n_blocks=n_blocks, n_heads=heads_per_step,
                             hd=hd, scale=hd ** -0.5, tq=tq)
    return pl.pallas_call(
        kern,
        grid=(batch, MOBA_HEADS // heads_per_step, n_tiles),
        in_specs=[
            pl.BlockSpec((tq, gw), lambda b, h, i: (b * n_tiles + i, h)),
            pl.BlockSpec((seq, gw), lambda b, h, i: (b, h)),
            pl.BlockSpec((seq, gw), lambda b, h, i: (b, h)),
        ],
        out_specs=pl.BlockSpec((tq, gw), lambda b, h, i: (b * n_tiles + i, h)),
        out_shape=jax.ShapeDtypeStruct((m, width), BF16),
        scratch_shapes=[pltpu.VMEM((heads_per_step, n_blocks, MOBA_BLOCK, 2 * hd), BF16),
                        pltpu.VMEM((heads_per_step, n_blocks, hd, MOBA_BLOCK), BF16),
                        pltpu.VMEM((heads_per_step, n_blocks, hd), F32)],
        compiler_params=_params("parallel", "parallel", "arbitrary"),
        name="moba_prompt",
    )(q, k, v)


def _decode_select_kernel(q_ref, kmean_ref, o_ref):
    gate = jnp.sum(kmean_ref[...] * q_ref[...], axis=-1, keepdims=True)
    _, picks = _top_blocks(gate, jnp.full(gate.shape, True), axis=0)
    for r, first in enumerate(picks):
        o_ref[r] = first[0]


def decode_select(q, kmean):
    batch, n_past, heads, hd = kmean.shape
    assert n_past >= MOBA_TOP_K
    return pl.pallas_call(
        _decode_select_kernel,
        grid=(batch,),
        in_specs=[pl.BlockSpec((None, heads, hd), lambda b: (b, 0, 0)),
                  pl.BlockSpec((None, n_past, heads, hd), lambda b: (b, 0, 0, 0))],
        out_specs=pl.BlockSpec((None, MOBA_TOP_K, heads, 1), lambda b: (b, 0, 0, 0)),
        out_shape=jax.ShapeDtypeStruct((batch, MOBA_TOP_K, heads, 1), jnp.int32),
        compiler_params=_params("parallel"),
        name="moba_decode_select",
    )(q, kmean)


def _decode_attend_kernel(pt_ref, top_ref, q_ref, kn_ref, vn_ref, ck_ref, cv_ref, o_ref,
                          kbuf_ref, vbuf_ref, sem_ref, *, heads, pages_per_block, n_pages_seq,
                          scale):
    b = pl.program_id(0)
    n_slots = MOBA_TOP_K * pages_per_block

    def head_copies(h):
        copies = []
        for r in range(MOBA_TOP_K):
            block = top_ref[(b * MOBA_TOP_K + r) * heads + h]
            for pg in range(pages_per_block):
                page = pt_ref[b * n_pages_seq + block * pages_per_block + pg]
                slot = r * pages_per_block + pg
                copies.append(pltpu.make_async_copy(
                    ck_ref.at[page, :, h, :], kbuf_ref.at[h, slot], sem_ref.at[0, h]))
                copies.append(pltpu.make_async_copy(
                    cv_ref.at[page, :, h, :], vbuf_ref.at[h, slot], sem_ref.at[1, h]))
        return copies

    all_copies = [head_copies(h) for h in range(heads)]
    for copies in all_copies:
        for cp in copies:
            cp.start()

    for h in range(heads):
        for cp in all_copies[h]:
            cp.wait()
        q = q_ref[h:h + 1, :]
        k_new, v_new = kn_ref[h:h + 1, :], vn_ref[h:h + 1, :]
        s_own = jnp.sum(q * k_new, axis=-1, keepdims=True) * scale
        s_past = [jnp.sum(kbuf_ref[h, i] * q, axis=-1, keepdims=True) * scale
                  for i in range(n_slots)]
        m = s_own
        for s in s_past:
            m = jnp.maximum(m, jnp.max(s, axis=0, keepdims=True))
        p_own = jnp.exp(s_own - m)
        denom = p_own
        acc = p_own * v_new
        for i, s in enumerate(s_past):
            p = jnp.exp(s - m)
            denom = denom + jnp.sum(p, axis=0, keepdims=True)
            acc = acc + jnp.sum(p * vbuf_ref[h, i], axis=0, keepdims=True)
        o_ref[h:h + 1, :] = acc / denom


def decode_attend(q, k_new, v_new, cache_k, cache_v, page_table_flat, top_flat, *, n_pages_seq):
    batch, heads, hd = q.shape
    page = cache_k.shape[1]
    ppb = MOBA_BLOCK // page
    n_slots = MOBA_TOP_K * ppb
    tok_spec = pl.BlockSpec((None, heads, hd), lambda b, pt, tp: (b, 0, 0))
    hbm_spec = pl.BlockSpec(memory_space=pl.ANY)
    return pl.pallas_call(
        functools.partial(_decode_attend_kernel, heads=heads, pages_per_block=ppb,
                          n_pages_seq=n_pages_seq, scale=hd ** -0.5),
        grid_spec=pltpu.PrefetchScalarGridSpec(
            num_scalar_prefetch=2,
            grid=(batch,),
            in_specs=[tok_spec, tok_spec, tok_spec, hbm_spec, hbm_spec],
            out_specs=tok_spec,
            scratch_shapes=[pltpu.VMEM((heads, n_slots, page, hd), F32),
                            pltpu.VMEM((heads, n_slots, page, hd), F32),
                            pltpu.SemaphoreType.DMA((2, heads))],
        ),
        out_shape=jax.ShapeDtypeStruct((batch, heads, hd), F32),
        compiler_params=_params("arbitrary"),
        name="moba_decode_attend",
    )(page_table_flat, top_flat, q, k_new, v_new, cache_k, cache_v)


def _weights_needed_after(layer, depth, n_gla):
    if layer + 1 >= depth:
        return []
    nxt = layer + 1
    needed = [("up", nxt), ("down", nxt)]
    if nxt < n_gla:
        needed.append(("out", nxt))
    else:
        needed += [("q", nxt - n_gla), ("o", nxt - n_gla)]
    if layer == n_gla - 1:
        needed.append(("kv", 0))
    return needed


def _trunk(x, w, bf16_w, *, batch, seq, raw_w=None, gla_s0=None, cache=None, kmean=None,
           kmean_side=None):
    n_gla, depth = w["n_gla"], w["depth"]
    heads = MOBA_HEADS
    h = x
    width = x.shape[1]
    hd = width // heads
    states = []
    kmean_parts = []
    k_new = v_new = None
    for l in range(depth):
        if l < n_gla:
            proj, = norm_matmul(h, w["norm_mix"][l], bf16_w["in", l], 0, n_cols=w["n_qkvr"],
                                tn=1536)
            b = gla_gate(h, w["norm_mix"][l], w["w_z"], w["w_gate"], w["b_gate"], l,
                         chunk=min(GLA_CHUNK, seq))
            if cache is None:
                todo = [("up", l), ("down", l)] if raw_w is not None and l == 0 else []
                og, s, *cast = gla_prompt(proj, b, w["head_norm"], l, batch=batch, seq=seq,
                                          casts=[(raw_w[name], index) for name, index in todo])
                bf16_w.update(zip(todo, cast))
            else:
                og, s = gla_step(proj, b, w["head_norm"], gla_s0, l)
            states.append(s)
            h = matmul_residual(og, bf16_w["out", l], h, 0)
        else:
            i = l - n_gla
            q, = norm_matmul(h, w["norm_mix"][l], bf16_w["q", i], 0, tm=512, tn=2048)
            if cache is None:
                o = moba_prompt(q, k_new, v_new, batch=batch, seq=seq)
            else:
                cache_k, cache_v, page_table_flat, n_pages = cache
                q3 = q.reshape(batch, heads, hd)
                top = decode_select(q3, kmean)
                o = decode_attend(q3, k_new.reshape(batch, heads, hd),
                                  v_new.reshape(batch, heads, hd), cache_k, cache_v,
                                  page_table_flat, top.reshape(-1),
                                  n_pages_seq=n_pages).reshape(batch, width)
            h = matmul_residual(o, bf16_w["o", i], h, 0)
        todo = _weights_needed_after(l, depth, n_gla) if raw_w is not None else []
        side = None
        if kmean_side is not None:
            side = (*kmean_side, sum(part.shape[0] for part in kmean_parts))
        results = list(mlp_residual(
            h, w["norm_mlp"][l:l + 1], bf16_w["up", l], bf16_w["down", l], 0,
            final_gain=w["norm_final"] if l == depth - 1 else None,
            casts=[(raw_w[name], index) for name, index in todo], kmean_side=side))
        h = results.pop(0)
        if side is not None:
            kmean_parts.append(results.pop(0))
        for key, cast in zip(todo, results):
            bf16_w[key] = cast
        if l == n_gla - 1:
            k_new, v_new = norm_matmul(h, w["kv_norm"], bf16_w["kv", 0], 0, n_split=2, tn=512)
    return h, jnp.stack(states), k_new, v_new, kmean_parts


def kernel(x_prompt, x_sample, state_gla, cache_k, cache_v, page_table, norm_mix, norm_mlp,
           w_mlp_up, w_mlp_down, w_in_a, w_gate_a, b_gate_a, head_norm_a, w_out_a, kv_norm,
           w_kv, w_q_b, w_o_b, norm_final):
    batch, seq, d = x_prompt.shape
    dec_batch, dec_seq, _ = x_sample.shape
    assert dec_seq == 1
    n_qkvr = w_in_a.shape[2] - GLA_GATE_RANK
    w = dict(
        depth=w_mlp_up.shape[0], n_gla=w_in_a.shape[0], n_qkvr=n_qkvr,
        norm_mix=norm_mix, norm_mlp=norm_mlp, norm_final=norm_final, kv_norm=kv_norm,
        head_norm=head_norm_a, w_gate=w_gate_a, b_gate=b_gate_a, w_z=w_in_a[:, :, n_qkvr:],
    )
    raw_w = {"out": w_out_a, "up": w_mlp_up, "down": w_mlp_down, "kv": w_kv[None],
             "q": w_q_b, "o": w_o_b}
    bf16_w = {("out", 0): w_out_a[0:1].astype(BF16)}
    for l in range(w["n_gla"]):
        bf16_w["in", l] = w_in_a[l:l + 1].astype(BF16)
    heads, hd = cache_k.shape[2], cache_k.shape[3]

    page_table_flat = page_table.reshape(-1)
    n_pages = page_table.shape[1]
    n_past = n_pages * cache_k.shape[1] // MOBA_BLOCK
    y_p, s_p, k_p, v_p, kmean_parts = _trunk(
        x_prompt.reshape(batch * seq, d), w, bf16_w, batch=batch, seq=seq, raw_w=raw_w,
        kmean_side=(cache_k, page_table_flat))
    kmean = jnp.concatenate(kmean_parts, axis=0)
    assert kmean.shape[0] == dec_batch * n_past, "prompt MLP grid steps must cover the cached blocks"
    cache = (cache_k, cache_v, page_table_flat, n_pages)
    y_s, s_s, k_s, v_s, _ = _trunk(
        x_sample.reshape(dec_batch, d), w, bf16_w, batch=dec_batch, seq=1, gla_s0=state_gla,
        cache=cache, kmean=kmean.reshape(dec_batch, n_past, heads, hd))
    return (y_p.reshape(batch, seq, d), y_s.reshape(dec_batch, 1, d), s_p, s_s,
            k_p.reshape(batch, seq, heads, hd), v_p.reshape(batch, seq, heads, hd),
            k_s.reshape(dec_batch, 1, heads, hd), v_s.reshape(dec_batch, 1, heads, hd))
```

```python
import functools

import jax
import jax.numpy as jnp
from jax import lax
from jax.experimental import pallas as pl
from jax.experimental.pallas import tpu as pltpu

F32 = jnp.float32
BF16 = jnp.bfloat16
HIGHEST = lax.Precision.HIGHEST

RMS_EPS = 1e-6
GLA_HEADS = 4
GLA_GATE_RANK = 16
GLA_GATE_TAU = 16.0
GLA_CHUNK = 64
GLA_SUB = 16
MOBA_HEADS = 16
MOBA_BLOCK = 256
MOBA_TOP_K = 3
LOG2_E = 1.4426950408889634
MASK_BIAS = 2.0 ** 100

V7X_VMEM_BYTES = 64 * 1024 * 1024
VMEM_LIMIT_BYTES = V7X_VMEM_BYTES - 8 * 1024 * 1024
BF16_SUBLANES = 16


def _params(*semantics):
    return pltpu.CompilerParams(dimension_semantics=semantics,
                                vmem_limit_bytes=VMEM_LIMIT_BYTES)


def _rms_normalize(x, gain):
    ms = jnp.mean(x * x, axis=-1, keepdims=True)
    return x * lax.rsqrt(ms + RMS_EPS) * gain


def _row_tile(m, want):
    return want if m % want == 0 else m


def _col_tile(n, want):
    while n % want:
        want //= 2
    return want


def _contract_last(a, b, **kw):
    return lax.dot_general(a, b, (((1,), (1,)), ((), ())), preferred_element_type=F32, **kw)


def _contract_first(a, b):
    return lax.dot_general(a, b, (((0,), (0,)), ((), ())), preferred_element_type=F32)


def _transpose_cast_kernel(wt_ref, o_ref):
    o_ref[...] = wt_ref[...].T.astype(o_ref.dtype)


def transpose_cast(w_t, n_cols, *, tn=512):
    layers, _, d = w_t.shape
    assert n_cols % tn == 0
    return pl.pallas_call(
        _transpose_cast_kernel,
        grid=(layers, n_cols // tn),
        in_specs=[pl.BlockSpec((None, tn, d), lambda l, j: (l, j, 0))],
        out_specs=pl.BlockSpec((None, d, tn), lambda l, j: (l, 0, j)),
        out_shape=jax.ShapeDtypeStruct((layers, d, n_cols), BF16),
        compiler_params=_params("parallel", "parallel"),
        name="transpose_cast",
    )(w_t)


def _norm_matmul_kernel(x_ref, gain_ref, *refs, n_split):
    w_refs, o_refs, xn_ref = refs[:n_split], refs[n_split:2 * n_split], refs[2 * n_split]

    @pl.when(pl.program_id(1) == 0)
    def _():
        xn_ref[...] = _rms_normalize(x_ref[...], gain_ref[...]).astype(xn_ref.dtype)

    for w_ref, o_ref in zip(w_refs, o_refs):
        o_ref[...] = jnp.dot(xn_ref[...], w_ref[...], preferred_element_type=F32)


def norm_matmul(x, gain, w, layer, *, n_split=1, tm=1024, tn=1024):
    m, d = x.shape
    n = w.shape[2]
    n_out = n // n_split
    tm = _row_tile(m, tm)
    tn = _col_tile(n_out, tn)
    per_split = n_out // tn

    def w_spec(s):
        return pl.BlockSpec((None, d, tn), lambda i, j: (layer, 0, s * per_split + j))

    return pl.pallas_call(
        functools.partial(_norm_matmul_kernel, n_split=n_split),
        grid=(m // tm, per_split),
        in_specs=[pl.BlockSpec((tm, d), lambda i, j: (i, 0)),
                  pl.BlockSpec((1, d), lambda i, j: (0, 0))]
                 + [w_spec(s) for s in range(n_split)],
        out_specs=[pl.BlockSpec((tm, tn), lambda i, j: (i, j))] * n_split,
        out_shape=[jax.ShapeDtypeStruct((m, n_out), F32)] * n_split,
        scratch_shapes=[pltpu.VMEM((tm, d), BF16)],
        compiler_params=_params("parallel", "arbitrary"),
        name="norm_matmul",
    )(x, gain.reshape(1, d), *([w] * n_split))


def _split_bf16(a):
    hi = a.astype(BF16)
    return hi, a - hi.astype(F32)


def _dot_split(a, b):
    a_hi, a_rest = _split_bf16(a)
    b_hi, b_rest = _split_bf16(b)
    a_lo, b_lo = a_rest.astype(BF16), b_rest.astype(BF16)
    return (jnp.dot(a_hi, b_hi, preferred_element_type=F32)
            + jnp.dot(a_lo, b_hi, preferred_element_type=F32)
            + jnp.dot(a_hi, b_lo, preferred_element_type=F32))


def _gla_gate_kernel(x_ref, gain_ref, wz_ref, wg_ref, bg_ref, o_ref, *, chunk):
    xn = _rms_normalize(x_ref[...], gain_ref[...])
    z = jnp.dot(xn.astype(BF16), wz_ref[...].astype(BF16), preferred_element_type=F32)
    logit = _dot_split(z, wg_ref[...]) + bg_ref[...]
    log_sig = jnp.minimum(logit, 0.0) - jnp.log(1.0 + jnp.exp(-jnp.abs(logit)))
    g = log_sig / GLA_GATE_TAU
    if chunk > 1:
        rows = g.shape[0]
        row = lax.broadcasted_iota(jnp.int32, (rows, rows), 0)
        col = lax.broadcasted_iota(jnp.int32, (rows, rows), 1)
        same_chunk_past = (row >= col) & (row // chunk == col // chunk)
        ones = jnp.where(same_chunk_past, 1.0, 0.0).astype(BF16)
        g_hi, g_rest = _split_bf16(g)
        g_mid, g_rest = _split_bf16(g_rest)
        g = (jnp.dot(ones, g_hi, preferred_element_type=F32)
             + jnp.dot(ones, g_mid, preferred_element_type=F32)
             + jnp.dot(ones, g_rest.astype(BF16), preferred_element_type=F32))
    o_ref[...] = g


def gla_gate(x, gain, w_z, w_gate, b_gate, layer, *, chunk, tm=256):
    m, d = x.shape
    rank, n = w_gate.shape[1:]
    tm = _row_tile(m, tm)
    assert chunk == 1 or tm % chunk == 0
    return pl.pallas_call(
        functools.partial(_gla_gate_kernel, chunk=chunk),
        grid=(m // tm,),
        in_specs=[
            pl.BlockSpec((tm, d), lambda i: (i, 0)),
            pl.BlockSpec((1, d), lambda i: (0, 0)),
            pl.BlockSpec((None, d, rank), lambda i: (layer, 0, 0)),
            pl.BlockSpec((None, rank, n), lambda i: (layer, 0, 0)),
            pl.BlockSpec((None, 1, n), lambda i: (layer, 0, 0)),
        ],
        out_specs=pl.BlockSpec((tm, n), lambda i: (i, 0)),
        out_shape=jax.ShapeDtypeStruct((m, n), F32),
        compiler_params=_params("parallel"),
        name="gla_gate",
    )(x, gain.reshape(1, d), w_z, w_gate, b_gate.reshape(b_gate.shape[0], 1, n))


def _gla_chunk(q, k, v, b, st):
    c, dk = q.shape
    n_sub = c // GLA_SUB
    row = lax.broadcasted_iota(jnp.int32, (c, c), 0)
    col = lax.broadcasted_iota(jnp.int32, (c, c), 1)

    def rows_of(r, n):
        return jnp.concatenate(
            [jnp.broadcast_to(b[j * GLA_SUB + r:j * GLA_SUB + r + 1, :], (GLA_SUB, dk))
             for j in range(n)], axis=0)

    b_end = b[c - 1:c, :]
    v_b = v.astype(BF16)

    k_end = (k * jnp.exp(b_end - b)).astype(BF16)
    st_new = st * jnp.exp(b_end) + _contract_first(v_b, k_end)

    from_start = b - rows_of(0, n_sub)
    q_diag = q * jnp.exp(from_start)
    k_diag = k * jnp.exp(-from_start)
    att_diag = _contract_last(q_diag.astype(BF16), k_diag.astype(BF16))
    same_sub = (row // GLA_SUB) == (col // GLA_SUB)
    att = jnp.where(same_sub & (row >= col), att_diag, 0.0)

    if n_sub > 1:
        past = c - GLA_SUB
        k_off = (k[:past] * jnp.exp(rows_of(GLA_SUB - 1, n_sub - 1) - b[:past])).astype(BF16)
        q_parts, k_parts = [], []
        for j in range(n_sub - 1):
            lo = (j + 1) * GLA_SUB
            q_j = (q[lo:] * jnp.exp(b[lo:] - b[lo - 1:lo, :])).astype(BF16)
            q_parts.append(jnp.concatenate([jnp.zeros((lo, dk), BF16), q_j], axis=0))
            pieces = [k_off[lo - GLA_SUB:lo], jnp.zeros((c - lo, dk), BF16)]
            if j:
                pieces.insert(0, jnp.zeros((lo - GLA_SUB, dk), BF16))
            k_parts.append(jnp.concatenate(pieces, axis=0))
        att = att + _contract_last(jnp.concatenate(q_parts, axis=1),
                                   jnp.concatenate(k_parts, axis=1))

    o = _contract_last((q * jnp.exp(b)).astype(BF16), st.astype(BF16))
    o = o + jnp.dot(att.astype(BF16), v_b, preferred_element_type=F32)
    return o, st_new


def _gla_prompt_kernel(q_ref, k_ref, v_ref, r_ref, b_ref, hn_ref, *refs, n_casts, n_chunks,
                       q_scale, unroll):
    cast_src, (og_ref, s_ref) = refs[:n_casts], refs[n_casts:n_casts + 2]
    cast_dst, st_ref = refs[n_casts + 2:2 * n_casts + 2], refs[2 * n_casts + 2]
    t = pl.program_id(2)
    for src, dst in zip(cast_src, cast_dst):
        dst[...] = src[...].astype(dst.dtype)

    @pl.when(t == 0)
    def _():
        st_ref[...] = jnp.zeros_like(st_ref)

    def chunk(ci, carry):
        rows = pl.ds(pl.multiple_of(ci * GLA_CHUNK, GLA_CHUNK), GLA_CHUNK)
        o, st_new = _gla_chunk(q_ref[rows, :] * q_scale, k_ref[rows, :], v_ref[rows, :],
                               b_ref[rows, :], st_ref[...])
        st_ref[...] = st_new
        r = r_ref[rows, :]
        silu = r / (1.0 + jnp.exp(-r))
        og_ref[rows, :] = (_rms_normalize(o, hn_ref[...]) * silu).astype(og_ref.dtype)
        return carry

    lax.fori_loop(0, n_chunks, chunk, 0, unroll=unroll)

    @pl.when(t == pl.num_programs(2) - 1)
    def _():
        s_ref[...] = st_ref[...].T


def _cast_side_stream(casts, grid):
    n_steps = 1
    for extent in grid:
        n_steps *= extent

    def step_of(*ids):
        step = 0
        for idx, extent in zip(ids, grid):
            step = step * extent + idx
        return step

    operands, in_specs, out_specs, out_shapes = [], [], [], []
    for src, src_layer in casts:
        _, rows, cols = src.shape
        slab = rows // n_steps
        assert rows == slab * n_steps and slab % BF16_SUBLANES == 0
        operands.append(src)
        in_specs.append(pl.BlockSpec(
            (None, slab, cols),
            lambda *ids, src_layer=src_layer: (src_layer, step_of(*ids[:len(grid)]), 0)))
        out_specs.append(pl.BlockSpec((None, slab, cols),
                                      lambda *ids: (0, step_of(*ids[:len(grid)]), 0)))
        out_shapes.append(jax.ShapeDtypeStruct((1, rows, cols), BF16))
    return operands, in_specs, out_specs, out_shapes


def gla_prompt(proj, b, head_norm, layer, *, batch, seq, tt=1024, unroll=8, casts=()):
    m = batch * seq
    qk = b.shape[1]
    dk = qk // GLA_HEADS
    vw = (proj.shape[1] - 2 * qk) // 2
    dv = vw // GLA_HEADS
    tt = min(tt, seq)
    assert seq % tt == 0 and tt % GLA_CHUNK == 0 and dv % dk == 0
    nt = seq // tt
    k_col0 = qk // dk
    v_col0 = 2 * qk // dv
    r_col0 = (2 * qk + vw) // dv
    grid = (batch, GLA_HEADS, nt)
    cast_operands, cast_in, cast_out, cast_shapes = _cast_side_stream(casts, grid)
    kern = functools.partial(_gla_prompt_kernel, n_casts=len(casts),
                             n_chunks=tt // GLA_CHUNK, q_scale=dk ** -0.5, unroll=unroll)
    return pl.pallas_call(
        kern,
        grid=grid,
        in_specs=[
            pl.BlockSpec((tt, dk), lambda bi, h, t: (bi * nt + t, h)),
            pl.BlockSpec((tt, dk), lambda bi, h, t: (bi * nt + t, k_col0 + h)),
            pl.BlockSpec((tt, dv), lambda bi, h, t: (bi * nt + t, v_col0 + h)),
            pl.BlockSpec((tt, dv), lambda bi, h, t: (bi * nt + t, r_col0 + h)),
            pl.BlockSpec((tt, dk), lambda bi, h, t: (bi * nt + t, h)),
            pl.BlockSpec((None, 1, dv), lambda bi, h, t: (layer, 0, 0)),
        ] + cast_in,
        out_specs=[
            pl.BlockSpec((tt, dv), lambda bi, h, t: (bi * nt + t, h)),
            pl.BlockSpec((None, None, dk, dv), lambda bi, h, t: (bi, h, 0, 0)),
        ] + cast_out,
        out_shape=[
            jax.ShapeDtypeStruct((m, vw), BF16),
            jax.ShapeDtypeStruct((batch, GLA_HEADS, dk, dv), F32),
        ] + cast_shapes,
        scratch_shapes=[pltpu.VMEM((dv, dk), F32)],
        compiler_params=_params("parallel", "parallel", "arbitrary"),
        name="gla_prompt",
    )(proj, proj, proj, proj, b, head_norm.reshape(head_norm.shape[0], 1, dv), *cast_operands)


def _gla_step_kernel(qkg_ref, v_ref, r_ref, hn_ref, s0_ref, og_ref, s_ref, *, q_scale):
    for h in range(GLA_HEADS):
        q, k, g = (qkg_ref[h, :, i:i + 1] for i in range(3))
        s_new = jnp.exp(g) * s0_ref[h] + k * v_ref[h]
        s_ref[h] = s_new
        o = jnp.sum((q * q_scale) * s_new, axis=0, keepdims=True)
        r = r_ref[h]
        silu = r / (1.0 + jnp.exp(-r))
        og_ref[h] = _rms_normalize(o, hn_ref[...]) * silu


def gla_step(proj, g, head_norm, s0, layer):
    batch = proj.shape[0]
    qk = g.shape[1]
    dk = qk // GLA_HEADS
    vw = (proj.shape[1] - 2 * qk) // 2
    dv = vw // GLA_HEADS
    qkg = jnp.stack([proj[:, :qk], proj[:, qk:2 * qk], g], axis=-1)
    qkg = qkg.reshape(batch, GLA_HEADS, dk, 3)
    row = lambda a: a.reshape(batch, GLA_HEADS, 1, dv)
    v_r = row(proj[:, 2 * qk:2 * qk + vw])
    r_r = row(proj[:, 2 * qk + vw:])
    row_spec = pl.BlockSpec((None, GLA_HEADS, 1, dv), lambda bi: (bi, 0, 0, 0))
    og, s = pl.pallas_call(
        functools.partial(_gla_step_kernel, q_scale=dk ** -0.5),
        grid=(batch,),
        in_specs=[pl.BlockSpec((None, GLA_HEADS, dk, 3), lambda bi: (bi, 0, 0, 0)),
                  row_spec, row_spec,
                  pl.BlockSpec((None, 1, dv), lambda bi: (layer, 0, 0)),
                  pl.BlockSpec((None, None, GLA_HEADS, dk, dv), lambda bi: (layer, bi, 0, 0, 0))],
        out_specs=[row_spec,
                   pl.BlockSpec((None, GLA_HEADS, dk, dv), lambda bi: (bi, 0, 0, 0))],
        out_shape=[jax.ShapeDtypeStruct((batch, GLA_HEADS, 1, dv), F32),
                   jax.ShapeDtypeStruct(s0.shape[1:], F32)],
        compiler_params=_params("parallel"),
        name="gla_step",
    )(qkg, v_r, r_r, head_norm.reshape(head_norm.shape[0], 1, dv), s0)
    return og.reshape(batch, vw), s


def _matmul_residual_kernel(a_ref, w_ref, h_ref, o_ref):
    o_ref[...] = h_ref[...] + jnp.dot(a_ref[...].astype(BF16), w_ref[...],
                                      preferred_element_type=F32)


def matmul_residual(a, w, h, layer, *, tm=512, tn=2048):
    m, kd = a.shape
    n = w.shape[2]
    tm = _row_tile(m, tm)
    tn = _col_tile(n, tn)
    return pl.pallas_call(
        _matmul_residual_kernel,
        grid=(m // tm, n // tn),
        in_specs=[
            pl.BlockSpec((tm, kd), lambda i, j: (i, 0)),
            pl.BlockSpec((None, kd, tn), lambda i, j: (layer, 0, j)),
            pl.BlockSpec((tm, tn), lambda i, j: (i, j)),
        ],
        out_specs=pl.BlockSpec((tm, tn), lambda i, j: (i, j)),
        out_shape=jax.ShapeDtypeStruct((m, n), F32),
        compiler_params=_params("parallel", "parallel"),
        name="matmul_residual",
    )(a, w, h)


def _mlp_kernel(*refs, has_final_gain, n_casts, kmean):
    refs = list(refs)
    pt_ref = refs.pop(0) if kmean else None
    h_ref, gain_ref, wu_ref, wd_ref = refs[:4]
    del refs[:4]
    final_gain_ref = refs.pop(0) if has_final_gain else None
    cast_src = [refs.pop(0) for _ in range(n_casts)]
    ck_ref = refs.pop(0) if kmean else None
    o_ref = refs.pop(0)
    km_ref = refs.pop(0) if kmean else None
    cast_dst = [refs.pop(0) for _ in range(n_casts)]
    xn_ref = refs.pop(0)

    f, n_f = pl.program_id(1), pl.num_programs(1)
    if kmean:
        first_block, pages_per_block, block_len = kmean
        pages_ref, sem_ref = refs
        step = pl.program_id(0) * n_f + f
        slot = step % 2

        def page_copies(s, sl):
            first_page = (first_block + s) * pages_per_block
            return [pltpu.make_async_copy(ck_ref.at[pt_ref[first_page + pg]],
                                          pages_ref.at[sl, pg], sem_ref.at[sl])
                    for pg in range(pages_per_block)]

        @pl.when(step == 0)
        def _():
            for cp in page_copies(0, 0):
                cp.start()

        @pl.when(step + 1 < pl.num_programs(0) * n_f)
        def _():
            for cp in page_copies(step + 1, 1 - slot):
                cp.start()

        for cp in page_copies(step, slot):
            cp.wait()

    @pl.when(f == 0)
    def _():
        h = h_ref[...]
        xn_ref[...] = _rms_normalize(h, gain_ref[...]).astype(xn_ref.dtype)
        o_ref[...] = h

    u = jnp.maximum(jnp.dot(xn_ref[...], wu_ref[...], preferred_element_type=F32), 0.0)
    o_ref[...] += jnp.dot((u * u).astype(BF16), wd_ref[...], preferred_element_type=F32)

    if has_final_gain:
        @pl.when(f == n_f - 1)
        def _():
            o_ref[...] = _rms_normalize(o_ref[...], final_gain_ref[...])

    for src, dst in zip(cast_src, cast_dst):
        dst[...] = src[...].astype(dst.dtype)

    if kmean:
        total = jnp.sum(pages_ref[slot, 0], axis=0)
        for pg in range(1, pages_per_block):
            total = total + jnp.sum(pages_ref[slot, pg], axis=0)
        km_ref[...] = total / block_len


def mlp_residual(h, gain, w_up, w_down, layer, *, tm=512, tf=1024, final_gain=None,
                 casts=(), kmean_side=None):
    m, d = h.shape
    ff = w_up.shape[2]
    tm = _row_tile(m, tm)
    tf = _col_tile(ff, tf)
    grid = (m // tm, ff // tf)
    n_steps = grid[0] * grid[1]
    step_of = lambda i, f: i * grid[1] + f

    operands = [h, gain.reshape(gain.shape[0], 1, d), w_up, w_down]
    in_specs = [
        pl.BlockSpec((tm, d), lambda i, f, *_: (i, 0)),
        pl.BlockSpec((None, 1, d), lambda i, f, *_: (layer, 0, 0)),
        pl.BlockSpec((None, d, tf), lambda i, f, *_: (layer, 0, f)),
        pl.BlockSpec((None, tf, d), lambda i, f, *_: (layer, f, 0)),
    ]
    out_specs = [pl.BlockSpec((tm, d), lambda i, f, *_: (i, 0))]
    out_shapes = [jax.ShapeDtypeStruct((m, d), F32)]
    scratch = [pltpu.VMEM((tm, d), BF16)]
    if final_gain is not None:
        operands.append(final_gain.reshape(1, d))
        in_specs.append(pl.BlockSpec((1, d), lambda i, f, *_: (0, 0)))
    cast_operands, cast_in, cast_out_specs, cast_out_shapes = _cast_side_stream(casts, grid)
    operands += cast_operands
    in_specs += cast_in
    kmean = None
    if kmean_side is not None:
        cache_k, page_table_flat, first_block = kmean_side
        _, page, heads, hd = cache_k.shape
        ppb = MOBA_BLOCK // page
        kmean = (first_block, ppb, MOBA_BLOCK)
        operands = [page_table_flat] + operands + [cache_k]
        in_specs.append(pl.BlockSpec(memory_space=pl.ANY))
        out_specs.append(pl.BlockSpec((None, heads, hd), lambda i, f, *_: (step_of(i, f), 0, 0)))
        out_shapes.append(jax.ShapeDtypeStruct((n_steps, heads, hd), F32))
        scratch += [pltpu.VMEM((2, ppb, page, heads, hd), F32), pltpu.SemaphoreType.DMA((2,))]
    kern = functools.partial(_mlp_kernel, has_final_gain=final_gain is not None,
                             n_casts=len(casts), kmean=kmean)
    return pl.pallas_call(
        kern,
        grid_spec=pltpu.PrefetchScalarGridSpec(
            num_scalar_prefetch=1 if kmean else 0, grid=grid, in_specs=in_specs,
            out_specs=out_specs + cast_out_specs, scratch_shapes=scratch),
        out_shape=out_shapes + cast_out_shapes,
        compiler_params=_params("arbitrary" if kmean else "parallel", "arbitrary"),
        name="mlp_residual",
    )(*operands)


def _top_blocks(gate, is_past, axis):
    n_blocks = gate.shape[axis]
    idx = lax.broadcasted_iota(jnp.int32, gate.shape, axis)
    neg_inf = jnp.float32(-jnp.inf)
    left = jnp.where(is_past, gate, neg_inf)
    chosen = jnp.zeros(gate.shape, dtype=jnp.bool_)
    picks = []
    for _ in range(MOBA_TOP_K):
        best = jnp.max(left, axis=axis, keepdims=True)
        is_best = (left == best) & (best > neg_inf)
        first = jnp.min(jnp.where(is_best, idx, n_blocks), axis=axis, keepdims=True)
        pick = idx == first
        chosen = chosen | pick
        left = jnp.where(pick, neg_inf, left)
        picks.append(first)
    return chosen, picks


def _moba_prompt_kernel(q_ref, k_ref, v_ref, o_ref, kb_ref, vt_ref, kmean_ref,
                        *, n_blocks, n_heads, hd, scale, tq):
    ti = pl.program_id(2)
    blk = MOBA_BLOCK
    heads = range(n_heads)
    own_blocks = tq // blk

    @pl.when(ti == 0)
    def _():
        feature = lax.broadcasted_iota(jnp.int32, (blk, hd), 1)
        for j in heads:
            cols = slice(j * hd, (j + 1) * hd)
            for n in range(n_blocks):
                k_blk = k_ref[n * blk:(n + 1) * blk, cols]
                one_hot = jnp.where(feature == n, 1.0, 0.0).astype(BF16)
                kb_ref[j, n] = jnp.concatenate([k_blk.astype(BF16), one_hot], axis=1)
                vt_ref[j, n] = v_ref[n * blk:(n + 1) * blk, cols].T.astype(BF16)
                kmean_ref[j, n:n + 1, :] = jnp.mean(k_blk, axis=0, keepdims=True)

    q_pos = ti * tq + lax.broadcasted_iota(jnp.int32, (1, tq), 1)
    q_block = q_pos // blk
    block_id = lax.broadcasted_iota(jnp.int32, (n_blocks, tq), 0)
    key_off = lax.broadcasted_iota(jnp.int32, (blk, tq), 0)
    neg_inf = jnp.float32(-jnp.inf)

    q_bs, carry0 = [], []
    for j in heads:
        q = q_ref[:, j * hd:(j + 1) * hd]
        gate = _contract_last(kmean_ref[j], q, precision=HIGHEST)
        chosen, _ = _top_blocks(gate, block_id < q_block, axis=0)
        bias = jnp.where(chosen | (block_id == q_block), 0.0, -MASK_BIAS)
        bias = jnp.concatenate([bias, jnp.zeros((hd - n_blocks, tq), F32)], axis=0)
        q_bs.append(jnp.concatenate([(q * (scale * LOG2_E)).astype(BF16),
                                     bias.T.astype(BF16)], axis=1))
        carry0.append((jnp.full((1, tq), neg_inf), jnp.zeros((1, tq), F32),
                       jnp.zeros((hd, tq), F32)))

    def key_block(n, carry, causal):
        raw = [_contract_last(kb_ref[j, n], q_bs[j]) for j in heads]
        partial = []
        for j in heads:
            m_run, l_run, _ = carry[j]
            s = raw[j]
            if causal:
                s = jnp.where(key_off <= q_pos - n * blk, s, neg_inf)
            m_new = jnp.maximum(m_run, jnp.max(s, axis=0, keepdims=True))
            alpha = jnp.exp2(m_run - m_new)
            p = jnp.exp2(s - m_new)
            l_new = alpha * l_run + jnp.sum(p, axis=0, keepdims=True)
            partial.append((m_new, l_new, alpha, p.astype(BF16)))
        out = []
        for j in heads:
            m_new, l_new, alpha, p_b = partial[j]
            acc = alpha * carry[j][2] + jnp.dot(vt_ref[j, n], p_b,
                                                preferred_element_type=F32)
            out.append((m_new, l_new, acc))
        return tuple(out)

    first_own = ti * own_blocks
    carry = lax.fori_loop(0, first_own, functools.partial(key_block, causal=False),
                          tuple(carry0))
    for i in range(own_blocks):
        carry = key_block(first_own + i, carry, causal=True)
    for j in heads:
        _, l_fin, acc = carry[j]
        o_ref[:, j * hd:(j + 1) * hd] = (acc / l_fin).T.astype(o_ref.dtype)


def moba_prompt(q, k, v, *, batch, seq, heads_per_step=4, tq=512):
    m, width = q.shape
    hd = width // MOBA_HEADS
    tq = min(tq, seq)
    assert seq % tq == 0 and tq % MOBA_BLOCK == 0 and MOBA_HEADS % heads_per_step == 0
    n_blocks = seq // MOBA_BLOCK
    n_tiles = seq // tq
    gw = heads_per_step * hd
    kern = functools.partial(_moba_prompt_kernel, n_blocks=n_blocks, n_heads=heads_per_step,
                             hd=hd, scale=hd ** -0.5, tq=tq)
    return pl.pallas_call(
        kern,
        grid=(batch, MOBA_HEADS // heads_per_step, n_tiles),
        in_specs=[
            pl.BlockSpec((tq, gw), lambda b, h, i: (b * n_tiles + i, h)),
            pl.BlockSpec((seq, gw), lambda b, h, i: (b, h)),
            pl.BlockSpec((seq, gw), lambda b, h, i: (b, h)),
        ],
        out_specs=pl.BlockSpec((tq, gw), lambda b, h, i: (b * n_tiles + i, h)),
        out_shape=jax.ShapeDtypeStruct((m, width), BF16),
        scratch_shapes=[pltpu.VMEM((heads_per_step, n_blocks, MOBA_BLOCK, 2 * hd), BF16),
                        pltpu.VMEM((heads_per_step, n_blocks, hd, MOBA_BLOCK), BF16),
                        pltpu.VMEM((heads_per_step, n_blocks, hd), F32)],
        compiler_params=_params("parallel", "parallel", "arbitrary"),
        name="moba_prompt",
    )(q, k, v)


def _decode_select_kernel(q_ref, kmean_ref, o_ref):
    gate = jnp.sum(kmean_ref[...] * q_ref[...], axis=-1, keepdims=True)
    _, picks = _top_blocks(gate, jnp.full(gate.shape, True), axis=0)
    for r, first in enumerate(picks):
        o_ref[r] = first[0]


def decode_select(q, kmean):
    batch, n_past, heads, hd = kmean.shape
    assert n_past >= MOBA_TOP_K
    return pl.pallas_call(
        _decode_select_kernel,
        grid=(batch,),
        in_specs=[pl.BlockSpec((None, heads, hd), lambda b: (b, 0, 0)),
                  pl.BlockSpec((None, n_past, heads, hd), lambda b: (b, 0, 0, 0))],
        out_specs=pl.BlockSpec((None, MOBA_TOP_K, heads, 1), lambda b: (b, 0, 0, 0)),
        out_shape=jax.ShapeDtypeStruct((batch, MOBA_TOP_K, heads, 1), jnp.int32),
        compiler_params=_params("parallel"),
        name="moba_decode_select",
    )(q, kmean)


def _decode_attend_kernel(pt_ref, top_ref, q_ref, kn_ref, vn_ref, ck_ref, cv_ref, o_ref,
                          kbuf_ref, vbuf_ref, sem_ref, *, heads, pages_per_block, n_pages_seq,
                          scale):
    b = pl.program_id(0)
    n_slots = MOBA_TOP_K * pages_per_block

    def head_copies(h):
        copies = []
        for r in range(MOBA_TOP_K):
            block = top_ref[(b * MOBA_TOP_K + r) * heads + h]
            for pg in range(pages_per_block):
                page = pt_ref[b * n_pages_seq + block * pages_per_block + pg]
                slot = r * pages_per_block + pg
                copies.append(pltpu.make_async_copy(
                    ck_ref.at[page, :, h, :], kbuf_ref.at[h, slot], sem_ref.at[0, h]))
                copies.append(pltpu.make_async_copy(
                    cv_ref.at[page, :, h, :], vbuf_ref.at[h, slot], sem_ref.at[1, h]))
        return copies

    all_copies = [head_copies(h) for h in range(heads)]
    for copies in all_copies:
        for cp in copies:
            cp.start()

    for h in range(heads):
        for cp in all_copies[h]:
            cp.wait()
        q = q_ref[h:h + 1, :]
        k_new, v_new = kn_ref[h:h + 1, :], vn_ref[h:h + 1, :]
        s_own = jnp.sum(q * k_new, axis=-1, keepdims=True) * scale
        s_past = [jnp.sum(kbuf_ref[h, i] * q, axis=-1, keepdims=True) * scale
                  for i in range(n_slots)]
        m = s_own
        for s in s_past:
            m = jnp.maximum(m, jnp.max(s, axis=0, keepdims=True))
        p_own = jnp.exp(s_own - m)
        denom = p_own
        acc = p_own * v_new
        for i, s in enumerate(s_past):
            p = jnp.exp(s - m)
            denom = denom + jnp.sum(p, axis=0, keepdims=True)
            acc = acc + jnp.sum(p * vbuf_ref[h, i], axis=0, keepdims=True)
        o_ref[h:h + 1, :] = acc / denom


def decode_attend(q, k_new, v_new, cache_k, cache_v, page_table_flat, top_flat, *, n_pages_seq):
    batch, heads, hd = q.shape
    page = cache_k.shape[1]
    ppb = MOBA_BLOCK // page
    n_slots = MOBA_TOP_K * ppb
    tok_spec = pl.BlockSpec((None, heads, hd), lambda b, pt, tp: (b, 0, 0))
    hbm_spec = pl.BlockSpec(memory_space=pl.ANY)
    return pl.pallas_call(
        functools.partial(_decode_attend_kernel, heads=heads, pages_per_block=ppb,
                          n_pages_seq=n_pages_seq, scale=hd ** -0.5),
        grid_spec=pltpu.PrefetchScalarGridSpec(
            num_scalar_prefetch=2,
            grid=(batch,),
            in_specs=[tok_spec, tok_spec, tok_spec, hbm_spec, hbm_spec],
            out_specs=tok_spec,
            scratch_shapes=[pltpu.VMEM((heads, n_slots, page, hd), F32),
                            pltpu.VMEM((heads, n_slots, page, hd), F32),
                            pltpu.SemaphoreType.DMA((2, heads))],
        ),
        out_shape=jax.ShapeDtypeStruct((batch, heads, hd), F32),
        compiler_params=_params("arbitrary"),
        name="moba_decode_attend",
    )(page_table_flat, top_flat, q, k_new, v_new, cache_k, cache_v)


def _weights_needed_after(layer, depth, n_gla):
    if layer + 1 >= depth:
        return []
    nxt = layer + 1
    needed = [("up", nxt), ("down", nxt)]
    if nxt < n_gla:
        needed.append(("out", nxt))
    else:
        needed += [("q", nxt - n_gla), ("o", nxt - n_gla)]
    if layer == n_gla - 1:
        needed.append(("kv", 0))
    return needed


def _trunk(x, w, bf16_w, *, batch, seq, raw_w=None, gla_s0=None, cache=None, kmean=None,
           kmean_side=None):
    n_gla, depth = w["n_gla"], w["depth"]
    heads = MOBA_HEADS
    h = x
    width = x.shape[1]
    hd = width // heads
    states = []
    kmean_parts = []
    k_new = v_new = None
    for l in range(depth):
        if l < n_gla:
            proj, = norm_matmul(h, w["norm_mix"][l], w["w_qkvr"], l, tn=1536)
            b = gla_gate(h, w["norm_mix"][l], w["w_z"], w["w_gate"], w["b_gate"], l,
                         chunk=min(GLA_CHUNK, seq))
            if cache is None:
                todo = [("up", l), ("down", l)] if raw_w is not None and l == 0 else []
                og, s, *cast = gla_prompt(proj, b, w["head_norm"], l, batch=batch, seq=seq,
                                          casts=[(raw_w[name], index) for name, index in todo])
                bf16_w.update(zip(todo, cast))
            else:
                og, s = gla_step(proj, b, w["head_norm"], gla_s0, l)
            states.append(s)
            h = matmul_residual(og, bf16_w["out", l], h, 0)
        else:
            i = l - n_gla
            q, = norm_matmul(h, w["norm_mix"][l], bf16_w["q", i], 0, tm=512, tn=2048)
            if cache is None:
                o = moba_prompt(q, k_new, v_new, batch=batch, seq=seq)
            else:
                cache_k, cache_v, page_table_flat, n_pages = cache
                q3 = q.reshape(batch, heads, hd)
                top = decode_select(q3, kmean)
                o = decode_attend(q3, k_new.reshape(batch, heads, hd),
                                  v_new.reshape(batch, heads, hd), cache_k, cache_v,
                                  page_table_flat, top.reshape(-1),
                                  n_pages_seq=n_pages).reshape(batch, width)
            h = matmul_residual(o, bf16_w["o", i], h, 0)
        todo = _weights_needed_after(l, depth, n_gla) if raw_w is not None else []
        side = None
        if kmean_side is not None:
            side = (*kmean_side, sum(part.shape[0] for part in kmean_parts))
        results = list(mlp_residual(
            h, w["norm_mlp"][l:l + 1], bf16_w["up", l], bf16_w["down", l], 0,
            final_gain=w["norm_final"] if l == depth - 1 else None,
            casts=[(raw_w[name], index) for name, index in todo], kmean_side=side))
        h = results.pop(0)
        if side is not None:
            kmean_parts.append(results.pop(0))
        for key, cast in zip(todo, results):
            bf16_w[key] = cast
        if l == n_gla - 1:
            k_new, v_new = norm_matmul(h, w["kv_norm"], bf16_w["kv", 0], 0, n_split=2, tn=512)
    return h, jnp.stack(states), k_new, v_new, kmean_parts


def kernel(x_prompt, x_sample, state_gla, cache_k, cache_v, page_table, norm_mix, norm_mlp,
           w_mlp_up, w_mlp_down, w_in_a, w_gate_a, b_gate_a, head_norm_a, w_out_a, kv_norm,
           w_kv, w_q_b, w_o_b, norm_final):
    batch, seq, d = x_prompt.shape
    dec_batch, dec_seq, _ = x_sample.shape
    assert dec_seq == 1
    n_qkvr = w_in_a.shape[2] - GLA_GATE_RANK
    w = dict(
        depth=w_mlp_up.shape[0], n_gla=w_in_a.shape[0], n_qkvr=n_qkvr,
        norm_mix=norm_mix, norm_mlp=norm_mlp, norm_final=norm_final, kv_norm=kv_norm,
        head_norm=head_norm_a, w_gate=w_gate_a, b_gate=b_gate_a, w_z=w_in_a[:, :, n_qkvr:],
    )
    raw_w = {"out": w_out_a, "up": w_mlp_up, "down": w_mlp_down, "kv": w_kv[None],
             "q": w_q_b, "o": w_o_b}
    w["w_qkvr"] = transpose_cast(jnp.swapaxes(w_in_a, 1, 2), n_qkvr)
    bf16_w = {("out", 0): w_out_a[0:1].astype(BF16)}
    heads, hd = cache_k.shape[2], cache_k.shape[3]

    page_table_flat = page_table.reshape(-1)
    n_pages = page_table.shape[1]
    n_past = n_pages * cache_k.shape[1] // MOBA_BLOCK
    y_p, s_p, k_p, v_p, kmean_parts = _trunk(
        x_prompt.reshape(batch * seq, d), w, bf16_w, batch=batch, seq=seq, raw_w=raw_w,
        kmean_side=(cache_k, page_table_flat))
    kmean = jnp.concatenate(kmean_parts, axis=0)
    assert kmean.shape[0] == dec_batch * n_past, "prompt MLP grid steps must cover the cached blocks"
    cache = (cache_k, cache_v, page_table_flat, n_pages)
    y_s, s_s, k_s, v_s, _ = _trunk(
        x_sample.reshape(dec_batch, d), w, bf16_w, batch=dec_batch, seq=1, gla_s0=state_gla,
        cache=cache, kmean=kmean.reshape(dec_batch, n_past, heads, hd))
    return (y_p.reshape(batch, seq, d), y_s.reshape(dec_batch, 1, d), s_p, s_s,
            k_p.reshape(batch, seq, heads, hd), v_p.reshape(batch, seq, heads, hd),
            k_s.reshape(dec_batch, 1, heads, hd), v_s.reshape(dec_batch, 1, heads, hd))
```

```python
import functools

import jax
import jax.numpy as jnp
from jax import lax
from jax.experimental import pallas as pl
from jax.experimental.pallas import tpu as pltpu

F32 = jnp.float32
BF16 = jnp.bfloat16
HIGHEST = lax.Precision.HIGHEST

RMS_EPS = 1e-6
GLA_HEADS = 4
GLA_GATE_RANK = 16
GLA_GATE_TAU = 16.0
GLA_CHUNK = 64
GLA_SUB = 16
MOBA_HEADS = 16
MOBA_BLOCK = 256
MOBA_TOP_K = 3
LOG2_E = 1.4426950408889634
MASK_BIAS = 2.0 ** 100

V7X_VMEM_BYTES = 64 * 1024 * 1024
VMEM_LIMIT_BYTES = V7X_VMEM_BYTES - 8 * 1024 * 1024
BF16_SUBLANES = 16


def _params(*semantics):
    return pltpu.CompilerParams(dimension_semantics=semantics,
                                vmem_limit_bytes=VMEM_LIMIT_BYTES)


def _rms_normalize(x, gain):
    ms = jnp.mean(x * x, axis=-1, keepdims=True)
    return x * lax.rsqrt(ms + RMS_EPS) * gain


def _row_tile(m, want):
    return want if m % want == 0 else m


def _col_tile(n, want):
    while n % want:
        want //= 2
    return want


def _contract_last(a, b, **kw):
    return lax.dot_general(a, b, (((1,), (1,)), ((), ())), preferred_element_type=F32, **kw)


def _contract_first(a, b):
    return lax.dot_general(a, b, (((0,), (0,)), ((), ())), preferred_element_type=F32)


def _transpose_cast_kernel(wt_ref, o_ref):
    o_ref[...] = wt_ref[...].T.astype(o_ref.dtype)


def transpose_cast(w_t, n_cols, *, tn=512):
    layers, _, d = w_t.shape
    assert n_cols % tn == 0
    return pl.pallas_call(
        _transpose_cast_kernel,
        grid=(layers, n_cols // tn),
        in_specs=[pl.BlockSpec((None, tn, d), lambda l, j: (l, j, 0))],
        out_specs=pl.BlockSpec((None, d, tn), lambda l, j: (l, 0, j)),
        out_shape=jax.ShapeDtypeStruct((layers, d, n_cols), BF16),
        compiler_params=_params("parallel", "parallel"),
        name="transpose_cast",
    )(w_t)


def _norm_matmul_kernel(x_ref, gain_ref, *refs, n_split):
    w_refs, o_refs, xn_ref = refs[:n_split], refs[n_split:2 * n_split], refs[2 * n_split]

    @pl.when(pl.program_id(1) == 0)
    def _():
        xn_ref[...] = _rms_normalize(x_ref[...], gain_ref[...]).astype(xn_ref.dtype)

    for w_ref, o_ref in zip(w_refs, o_refs):
        o_ref[...] = jnp.dot(xn_ref[...], w_ref[...], preferred_element_type=F32)


def norm_matmul(x, gain, w, layer, *, n_split=1, tm=1024, tn=1024):
    m, d = x.shape
    n = w.shape[2]
    n_out = n // n_split
    tm = _row_tile(m, tm)
    tn = _col_tile(n_out, tn)
    per_split = n_out // tn

    def w_spec(s):
        return pl.BlockSpec((None, d, tn), lambda i, j: (layer, 0, s * per_split + j))

    return pl.pallas_call(
        functools.partial(_norm_matmul_kernel, n_split=n_split),
        grid=(m // tm, per_split),
        in_specs=[pl.BlockSpec((tm, d), lambda i, j: (i, 0)),
                  pl.BlockSpec((1, d), lambda i, j: (0, 0))]
                 + [w_spec(s) for s in range(n_split)],
        out_specs=[pl.BlockSpec((tm, tn), lambda i, j: (i, j))] * n_split,
        out_shape=[jax.ShapeDtypeStruct((m, n_out), F32)] * n_split,
        scratch_shapes=[pltpu.VMEM((tm, d), BF16)],
        compiler_params=_params("parallel", "arbitrary"),
        name="norm_matmul",
    )(x, gain.reshape(1, d), *([w] * n_split))


def _split_bf16(a):
    hi = a.astype(BF16)
    return hi, a - hi.astype(F32)


def _gla_gate_kernel(x_ref, gain_ref, wz_ref, wg_ref, bg_ref, o_ref, *, chunk):
    xn = _rms_normalize(x_ref[...], gain_ref[...])
    z = jnp.dot(xn.astype(BF16), wz_ref[...].astype(BF16), preferred_element_type=F32)
    logit = jnp.dot(z.astype(BF16), wg_ref[...].astype(BF16),
                    preferred_element_type=F32) + bg_ref[...]
    log_sig = jnp.minimum(logit, 0.0) - jnp.log(1.0 + jnp.exp(-jnp.abs(logit)))
    g = log_sig / GLA_GATE_TAU
    if chunk > 1:
        rows = g.shape[0]
        row = lax.broadcasted_iota(jnp.int32, (rows, rows), 0)
        col = lax.broadcasted_iota(jnp.int32, (rows, rows), 1)
        same_chunk_past = (row >= col) & (row // chunk == col // chunk)
        ones = jnp.where(same_chunk_past, 1.0, 0.0).astype(BF16)
        g_hi, g_rest = _split_bf16(g)
        g_mid, g_rest = _split_bf16(g_rest)
        g = (jnp.dot(ones, g_hi, preferred_element_type=F32)
             + jnp.dot(ones, g_mid, preferred_element_type=F32)
             + jnp.dot(ones, g_rest.astype(BF16), preferred_element_type=F32))
    o_ref[...] = g


def gla_gate(x, gain, w_z, w_gate, b_gate, layer, *, chunk, tm=256):
    m, d = x.shape
    rank, n = w_gate.shape[1:]
    tm = _row_tile(m, tm)
    assert chunk == 1 or tm % chunk == 0
    return pl.pallas_call(
        functools.partial(_gla_gate_kernel, chunk=chunk),
        grid=(m // tm,),
        in_specs=[
            pl.BlockSpec((tm, d), lambda i: (i, 0)),
            pl.BlockSpec((1, d), lambda i: (0, 0)),
            pl.BlockSpec((None, d, rank), lambda i: (layer, 0, 0)),
            pl.BlockSpec((None, rank, n), lambda i: (layer, 0, 0)),
            pl.BlockSpec((None, 1, n), lambda i: (layer, 0, 0)),
        ],
        out_specs=pl.BlockSpec((tm, n), lambda i: (i, 0)),
        out_shape=jax.ShapeDtypeStruct((m, n), F32),
        compiler_params=_params("parallel"),
        name="gla_gate",
    )(x, gain.reshape(1, d), w_z, w_gate, b_gate.reshape(b_gate.shape[0], 1, n))


def _gla_chunk(q, k, v, b, st):
    c, dk = q.shape
    n_sub = c // GLA_SUB
    row = lax.broadcasted_iota(jnp.int32, (c, c), 0)
    col = lax.broadcasted_iota(jnp.int32, (c, c), 1)

    def rows_of(r, n):
        return jnp.concatenate(
            [jnp.broadcast_to(b[j * GLA_SUB + r:j * GLA_SUB + r + 1, :], (GLA_SUB, dk))
             for j in range(n)], axis=0)

    b_end = b[c - 1:c, :]
    v_b = v.astype(BF16)

    k_end = (k * jnp.exp(b_end - b)).astype(BF16)
    st_new = st * jnp.exp(b_end) + _contract_first(v_b, k_end)

    from_start = b - rows_of(0, n_sub)
    q_diag = q * jnp.exp(from_start)
    k_diag = k * jnp.exp(-from_start)
    att_diag = _contract_last(q_diag.astype(BF16), k_diag.astype(BF16))
    same_sub = (row // GLA_SUB) == (col // GLA_SUB)
    att = jnp.where(same_sub & (row >= col), att_diag, 0.0)

    if n_sub > 1:
        past = c - GLA_SUB
        k_off = (k[:past] * jnp.exp(rows_of(GLA_SUB - 1, n_sub - 1) - b[:past])).astype(BF16)
        q_parts, k_parts = [], []
        for j in range(n_sub - 1):
            lo = (j + 1) * GLA_SUB
            q_j = (q[lo:] * jnp.exp(b[lo:] - b[lo - 1:lo, :])).astype(BF16)
            q_parts.append(jnp.concatenate([jnp.zeros((lo, dk), BF16), q_j], axis=0))
            pieces = [k_off[lo - GLA_SUB:lo], jnp.zeros((c - lo, dk), BF16)]
            if j:
                pieces.insert(0, jnp.zeros((lo - GLA_SUB, dk), BF16))
            k_parts.append(jnp.concatenate(pieces, axis=0))
        att = att + _contract_last(jnp.concatenate(q_parts, axis=1),
                                   jnp.concatenate(k_parts, axis=1))

    o = _contract_last((q * jnp.exp(b)).astype(BF16), st.astype(BF16))
    o = o + jnp.dot(att.astype(BF16), v_b, preferred_element_type=F32)
    return o, st_new


def _gla_prompt_kernel(q_ref, k_ref, v_ref, r_ref, b_ref, hn_ref, *refs, n_casts, n_chunks,
                       q_scale, unroll):
    cast_src, (og_ref, s_ref) = refs[:n_casts], refs[n_casts:n_casts + 2]
    cast_dst, st_ref = refs[n_casts + 2:2 * n_casts + 2], refs[2 * n_casts + 2]
    t = pl.program_id(2)
    for src, dst in zip(cast_src, cast_dst):
        dst[...] = src[...].astype(dst.dtype)

    @pl.when(t == 0)
    def _():
        st_ref[...] = jnp.zeros_like(st_ref)

    def chunk(ci, carry):
        rows = pl.ds(pl.multiple_of(ci * GLA_CHUNK, GLA_CHUNK), GLA_CHUNK)
        o, st_new = _gla_chunk(q_ref[rows, :] * q_scale, k_ref[rows, :], v_ref[rows, :],
                               b_ref[rows, :], st_ref[...])
        st_ref[...] = st_new
        r = r_ref[rows, :]
        silu = r / (1.0 + jnp.exp(-r))
        og_ref[rows, :] = (_rms_normalize(o, hn_ref[...]) * silu).astype(og_ref.dtype)
        return carry

    lax.fori_loop(0, n_chunks, chunk, 0, unroll=unroll)

    @pl.when(t == pl.num_programs(2) - 1)
    def _():
        s_ref[...] = st_ref[...].T


def _cast_side_stream(casts, grid):
    n_steps = 1
    for extent in grid:
        n_steps *= extent

    def step_of(*ids):
        step = 0
        for idx, extent in zip(ids, grid):
            step = step * extent + idx
        return step

    operands, in_specs, out_specs, out_shapes = [], [], [], []
    for src, src_layer in casts:
        _, rows, cols = src.shape
        slab = rows // n_steps
        assert rows == slab * n_steps and slab % BF16_SUBLANES == 0
        operands.append(src)
        in_specs.append(pl.BlockSpec(
            (None, slab, cols),
            lambda *ids, src_layer=src_layer: (src_layer, step_of(*ids[:len(grid)]), 0)))
        out_specs.append(pl.BlockSpec((None, slab, cols),
                                      lambda *ids: (0, step_of(*ids[:len(grid)]), 0)))
        out_shapes.append(jax.ShapeDtypeStruct((1, rows, cols), BF16))
    return operands, in_specs, out_specs, out_shapes


def gla_prompt(proj, b, head_norm, layer, *, batch, seq, tt=1024, unroll=8, casts=()):
    m = batch * seq
    qk = b.shape[1]
    dk = qk // GLA_HEADS
    vw = (proj.shape[1] - 2 * qk) // 2
    dv = vw // GLA_HEADS
    tt = min(tt, seq)
    assert seq % tt == 0 and tt % GLA_CHUNK == 0 and dv % dk == 0
    nt = seq // tt
    k_col0 = qk // dk
    v_col0 = 2 * qk // dv
    r_col0 = (2 * qk + vw) // dv
    grid = (batch, GLA_HEADS, nt)
    cast_operands, cast_in, cast_out, cast_shapes = _cast_side_stream(casts, grid)
    kern = functools.partial(_gla_prompt_kernel, n_casts=len(casts),
                             n_chunks=tt // GLA_CHUNK, q_scale=dk ** -0.5, unroll=unroll)
    return pl.pallas_call(
        kern,
        grid=grid,
        in_specs=[
            pl.BlockSpec((tt, dk), lambda bi, h, t: (bi * nt + t, h)),
            pl.BlockSpec((tt, dk), lambda bi, h, t: (bi * nt + t, k_col0 + h)),
            pl.BlockSpec((tt, dv), lambda bi, h, t: (bi * nt + t, v_col0 + h)),
            pl.BlockSpec((tt, dv), lambda bi, h, t: (bi * nt + t, r_col0 + h)),
            pl.BlockSpec((tt, dk), lambda bi, h, t: (bi * nt + t, h)),
            pl.BlockSpec((None, 1, dv), lambda bi, h, t: (layer, 0, 0)),
        ] + cast_in,
        out_specs=[
            pl.BlockSpec((tt, dv), lambda bi, h, t: (bi * nt + t, h)),
            pl.BlockSpec((None, None, dk, dv), lambda bi, h, t: (bi, h, 0, 0)),
        ] + cast_out,
        out_shape=[
            jax.ShapeDtypeStruct((m, vw), BF16),
            jax.ShapeDtypeStruct((batch, GLA_HEADS, dk, dv), F32),
        ] + cast_shapes,
        scratch_shapes=[pltpu.VMEM((dv, dk), F32)],
        compiler_params=_params("parallel", "parallel", "arbitrary"),
        name="gla_prompt",
    )(proj, proj, proj, proj, b, head_norm.reshape(head_norm.shape[0], 1, dv), *cast_operands)


def _gla_step_kernel(qkg_ref, v_ref, r_ref, hn_ref, s0_ref, og_ref, s_ref, *, q_scale):
    for h in range(GLA_HEADS):
        q, k, g = (qkg_ref[h, :, i:i + 1] for i in range(3))
        s_new = jnp.exp(g) * s0_ref[h] + k * v_ref[h]
        s_ref[h] = s_new
        o = jnp.sum((q * q_scale) * s_new, axis=0, keepdims=True)
        r = r_ref[h]
        silu = r / (1.0 + jnp.exp(-r))
        og_ref[h] = _rms_normalize(o, hn_ref[...]) * silu


def gla_step(proj, g, head_norm, s0, layer):
    batch = proj.shape[0]
    qk = g.shape[1]
    dk = qk // GLA_HEADS
    vw = (proj.shape[1] - 2 * qk) // 2
    dv = vw // GLA_HEADS
    qkg = jnp.stack([proj[:, :qk], proj[:, qk:2 * qk], g], axis=-1)
    qkg = qkg.reshape(batch, GLA_HEADS, dk, 3)
    row = lambda a: a.reshape(batch, GLA_HEADS, 1, dv)
    v_r = row(proj[:, 2 * qk:2 * qk + vw])
    r_r = row(proj[:, 2 * qk + vw:])
    row_spec = pl.BlockSpec((None, GLA_HEADS, 1, dv), lambda bi: (bi, 0, 0, 0))
    og, s = pl.pallas_call(
        functools.partial(_gla_step_kernel, q_scale=dk ** -0.5),
        grid=(batch,),
        in_specs=[pl.BlockSpec((None, GLA_HEADS, dk, 3), lambda bi: (bi, 0, 0, 0)),
                  row_spec, row_spec,
                  pl.BlockSpec((None, 1, dv), lambda bi: (layer, 0, 0)),
                  pl.BlockSpec((None, None, GLA_HEADS, dk, dv), lambda bi: (layer, bi, 0, 0, 0))],
        out_specs=[row_spec,
                   pl.BlockSpec((None, GLA_HEADS, dk, dv), lambda bi: (bi, 0, 0, 0))],
        out_shape=[jax.ShapeDtypeStruct((batch, GLA_HEADS, 1, dv), F32),
                   jax.ShapeDtypeStruct(s0.shape[1:], F32)],
        compiler_params=_params("parallel"),
        name="gla_step",
    )(qkg, v_r, r_r, head_norm.reshape(head_norm.shape[0], 1, dv), s0)
    return og.reshape(batch, vw), s


def _matmul_residual_kernel(a_ref, w_ref, h_ref, o_ref):
    o_ref[...] = h_ref[...] + jnp.dot(a_ref[...].astype(BF16), w_ref[...],
                                      preferred_element_type=F32)


def matmul_residual(a, w, h, layer, *, tm=512, tn=2048):
    m, kd = a.shape
    n = w.shape[2]
    tm = _row_tile(m, tm)
    tn = _col_tile(n, tn)
    return pl.pallas_call(
        _matmul_residual_kernel,
        grid=(m // tm, n // tn),
        in_specs=[
            pl.BlockSpec((tm, kd), lambda i, j: (i, 0)),
            pl.BlockSpec((None, kd, tn), lambda i, j: (layer, 0, j)),
            pl.BlockSpec((tm, tn), lambda i, j: (i, j)),
        ],
        out_specs=pl.BlockSpec((tm, tn), lambda i, j: (i, j)),
        out_shape=jax.ShapeDtypeStruct((m, n), F32),
        compiler_params=_params("parallel", "parallel"),
        name="matmul_residual",
    )(a, w, h)


def _mlp_kernel(*refs, has_final_gain, n_casts, kmean):
    refs = list(refs)
    pt_ref = refs.pop(0) if kmean else None
    h_ref, gain_ref, wu_ref, wd_ref = refs[:4]
    del refs[:4]
    final_gain_ref = refs.pop(0) if has_final_gain else None
    cast_src = [refs.pop(0) for _ in range(n_casts)]
    ck_ref = refs.pop(0) if kmean else None
    o_ref = refs.pop(0)
    km_ref = refs.pop(0) if kmean else None
    cast_dst = [refs.pop(0) for _ in range(n_casts)]
    xn_ref = refs.pop(0)

    f, n_f = pl.program_id(1), pl.num_programs(1)
    if kmean:
        first_block, pages_per_block, block_len = kmean
        pages_ref, sem_ref = refs
        step = pl.program_id(0) * n_f + f
        slot = step % 2

        def page_copies(s, sl):
            first_page = (first_block + s) * pages_per_block
            return [pltpu.make_async_copy(ck_ref.at[pt_ref[first_page + pg]],
                                          pages_ref.at[sl, pg], sem_ref.at[sl])
                    for pg in range(pages_per_block)]

        @pl.when(step == 0)
        def _():
            for cp in page_copies(0, 0):
                cp.start()

        @pl.when(step + 1 < pl.num_programs(0) * n_f)
        def _():
            for cp in page_copies(step + 1, 1 - slot):
                cp.start()

        for cp in page_copies(step, slot):
            cp.wait()

    @pl.when(f == 0)
    def _():
        h = h_ref[...]
        xn_ref[...] = _rms_normalize(h, gain_ref[...]).astype(xn_ref.dtype)
        o_ref[...] = h

    u = jnp.maximum(jnp.dot(xn_ref[...], wu_ref[...], preferred_element_type=F32), 0.0)
    o_ref[...] += jnp.dot((u * u).astype(BF16), wd_ref[...], preferred_element_type=F32)

    if has_final_gain:
        @pl.when(f == n_f - 1)
        def _():
            o_ref[...] = _rms_normalize(o_ref[...], final_gain_ref[...])

    for src, dst in zip(cast_src, cast_dst):
        dst[...] = src[...].astype(dst.dtype)

    if kmean:
        total = jnp.sum(pages_ref[slot, 0], axis=0)
        for pg in range(1, pages_per_block):
            total = total + jnp.sum(pages_ref[slot, pg], axis=0)
        km_ref[...] = total / block_len


def mlp_residual(h, gain, w_up, w_down, layer, *, tm=512, tf=1024, final_gain=None,
                 casts=(), kmean_side=None):
    m, d = h.shape
    ff = w_up.shape[2]
    tm = _row_tile(m, tm)
    tf = _col_tile(ff, tf)
    grid = (m // tm, ff // tf)
    n_steps = grid[0] * grid[1]
    step_of = lambda i, f: i * grid[1] + f

    operands = [h, gain.reshape(gain.shape[0], 1, d), w_up, w_down]
    in_specs = [
        pl.BlockSpec((tm, d), lambda i, f, *_: (i, 0)),
        pl.BlockSpec((None, 1, d), lambda i, f, *_: (layer, 0, 0)),
        pl.BlockSpec((None, d, tf), lambda i, f, *_: (layer, 0, f)),
        pl.BlockSpec((None, tf, d), lambda i, f, *_: (layer, f, 0)),
    ]
    out_specs = [pl.BlockSpec((tm, d), lambda i, f, *_: (i, 0))]
    out_shapes = [jax.ShapeDtypeStruct((m, d), F32)]
    scratch = [pltpu.VMEM((tm, d), BF16)]
    if final_gain is not None:
        operands.append(final_gain.reshape(1, d))
        in_specs.append(pl.BlockSpec((1, d), lambda i, f, *_: (0, 0)))
    cast_operands, cast_in, cast_out_specs, cast_out_shapes = _cast_side_stream(casts, grid)
    operands += cast_operands
    in_specs += cast_in
    kmean = None
    if kmean_side is not None:
        cache_k, page_table_flat, first_block = kmean_side
        _, page, heads, hd = cache_k.shape
        ppb = MOBA_BLOCK // page
        kmean = (first_block, ppb, MOBA_BLOCK)
        operands = [page_table_flat] + operands + [cache_k]
        in_specs.append(pl.BlockSpec(memory_space=pl.ANY))
        out_specs.append(pl.BlockSpec((None, heads, hd), lambda i, f, *_: (step_of(i, f), 0, 0)))
        out_shapes.append(jax.ShapeDtypeStruct((n_steps, heads, hd), F32))
        scratch += [pltpu.VMEM((2, ppb, page, heads, hd), F32), pltpu.SemaphoreType.DMA((2,))]
    kern = functools.partial(_mlp_kernel, has_final_gain=final_gain is not None,
                             n_casts=len(casts), kmean=kmean)
    return pl.pallas_call(
        kern,
        grid_spec=pltpu.PrefetchScalarGridSpec(
            num_scalar_prefetch=1 if kmean else 0, grid=grid, in_specs=in_specs,
            out_specs=out_specs + cast_out_specs, scratch_shapes=scratch),
        out_shape=out_shapes + cast_out_shapes,
        compiler_params=_params("arbitrary" if kmean else "parallel", "arbitrary"),
        name="mlp_residual",
    )(*operands)


def _top_blocks(gate, is_past, axis):
    n_blocks = gate.shape[axis]
    idx = lax.broadcasted_iota(jnp.int32, gate.shape, axis)
    neg_inf = jnp.float32(-jnp.inf)
    left = jnp.where(is_past, gate, neg_inf)
    chosen = jnp.zeros(gate.shape, dtype=jnp.bool_)
    picks = []
    for _ in range(MOBA_TOP_K):
        best = jnp.max(left, axis=axis, keepdims=True)
        is_best = (left == best) & (best > neg_inf)
        first = jnp.min(jnp.where(is_best, idx, n_blocks), axis=axis, keepdims=True)
        pick = idx == first
        chosen = chosen | pick
        left = jnp.where(pick, neg_inf, left)
        picks.append(first)
    return chosen, picks


def _moba_prompt_kernel(q_ref, k_ref, v_ref, o_ref, kb_ref, vt_ref, kmean_ref,
                        *, n_blocks, n_heads, hd, scale, tq):
    ti = pl.program_id(2)
    blk = MOBA_BLOCK
    heads = range(n_heads)
    own_blocks = tq // blk

    @pl.when(ti == 0)
    def _():
        feature = lax.broadcasted_iota(jnp.int32, (blk, hd), 1)
        for j in heads:
            cols = slice(j * hd, (j + 1) * hd)
            for n in range(n_blocks):
                k_blk = k_ref[n * blk:(n + 1) * blk, cols]
                one_hot = jnp.where(feature == n, 1.0, 0.0).astype(BF16)
                kb_ref[j, n] = jnp.concatenate([k_blk.astype(BF16), one_hot], axis=1)
                vt_ref[j, n] = v_ref[n * blk:(n + 1) * blk, cols].T.astype(BF16)
                kmean_ref[j, n:n + 1, :] = jnp.mean(k_blk, axis=0, keepdims=True)

    q_pos = ti * tq + lax.broadcasted_iota(jnp.int32, (1, tq), 1)
    q_block = q_pos // blk
    block_id = lax.broadcasted_iota(jnp.int32, (n_blocks, tq), 0)
    key_off = lax.broadcasted_iota(jnp.int32, (blk, tq), 0)
    neg_inf = jnp.float32(-jnp.inf)

    q_bs, carry0 = [], []
    for j in heads:
        q = q_ref[:, j * hd:(j + 1) * hd]
        gate = _contract_last(kmean_ref[j], q, precision=HIGHEST)
        chosen, _ = _top_blocks(gate, block_id < q_block, axis=0)
        bias = jnp.where(chosen | (block_id == q_block), 0.0, -MASK_BIAS)
        bias = jnp.concatenate([bias, jnp.zeros((hd - n_blocks, tq), F32)], axis=0)
        q_bs.append(jnp.concatenate([(q * (scale * LOG2_E)).astype(BF16),
                                     bias.T.astype(BF16)], axis=1))
        carry0.append((jnp.full((1, tq), neg_inf), jnp.zeros((1, tq), F32),
                       jnp.zeros((hd, tq), F32)))

    def key_block(n, carry, causal):
        raw = [_contract_last(kb_ref[j, n], q_bs[j]) for j in heads]
        partial = []
        for j in heads:
            m_run, l_run, _ = carry[j]
            s = raw[j]
            if causal:
                s = jnp.where(key_off <= q_pos - n * blk, s, neg_inf)
            m_new = jnp.maximum(m_run, jnp.max(s, axis=0, keepdims=True))
            alpha = jnp.exp2(m_run - m_new)
            p = jnp.exp2(s - m_new)
            l_new = alpha * l_run + jnp.sum(p, axis=0, keepdims=True)
            partial.append((m_new, l_new, alpha, p.astype(BF16)))
        out = []
        for j in heads:
            m_new, l_new, alpha, p_b = partial[j]
            acc = alpha * carry[j][2] + jnp.dot(vt_ref[j, n], p_b,
                                                preferred_element_type=F32)
            out.append((m_new, l_new, acc))
        return tuple(out)

    first_own = ti * own_blocks
    carry = lax.fori_loop(0, first_own, functools.partial(key_block, causal=False),
                          tuple(carry0))
    for i in range(own_blocks):
        carry = key_block(first_own + i, carry, causal=True)
    for j in heads:
        _, l_fin, acc = carry[j]
        o_ref[:, j * hd:(j + 1) * hd] = (acc / l_fin).T.astype(o_ref.dtype)


def moba_prompt(q, k, v, *, batch, seq, heads_per_step=4, tq=512):
    m, width = q.shape
    hd = width // MOBA_HEADS
    tq = min(tq, seq)
    assert seq % tq == 0 and tq % MOBA_BLOCK == 0 and MOBA_HEADS % heads_per_step == 0
    n_blocks = seq // MOBA_BLOCK
    n_tiles = seq // tq
    gw = heads_per_step * hd
    kern = functools.partial(_moba_prompt_kernel, n_blocks=n_blocks, n_heads=heads_per_step,
                             hd=hd, scale=hd ** -0.5, tq=tq)
    return pl.pallas_call(
        kern,
        grid=(batch, MOBA_HEADS // heads_per_step, n_tiles),
        in_specs=[
            pl.BlockSpec((tq, gw), lambda b, h, i: (b * n_tiles + i, h)),
            pl.BlockSpec((seq, gw), lambda b, h, i: (b, h)),
            pl.BlockSpec((seq, gw), lambda b, h, i: (b, h)),
        ],
        out_specs=pl.BlockSpec((tq, gw), lambda b, h, i: (b * n_tiles + i, h)),
        out_shape=jax.ShapeDtypeStruct((m, width), BF16),
        scratch_shapes=[pltpu.VMEM((heads_per_step, n_blocks, MOBA_BLOCK, 2 * hd), BF16),
                        pltpu.VMEM((heads_per_step, n_blocks, hd, MOBA_BLOCK), BF16),
                        pltpu.VMEM((heads_per_step, n_blocks, hd), F32)],
        compiler_params=_params("parallel", "parallel", "arbitrary"),
        name="moba_prompt",
    )(q, k, v)


def _decode_select_kernel(q_ref, kmean_ref, o_ref):
    gate = jnp.sum(kmean_ref[...] * q_ref[...], axis=-1, keepdims=True)
    _, picks = _top_blocks(gate, jnp.full(gate.shape, True), axis=0)
    for r, first in enumerate(picks):
        o_ref[r] = first[0]


def decode_select(q, kmean):
    batch, n_past, heads, hd = kmean.shape
    assert n_past >= MOBA_TOP_K
    return pl.pallas_call(
        _decode_select_kernel,
        grid=(batch,),
        in_specs=[pl.BlockSpec((None, heads, hd), lambda b: (b, 0, 0)),
                  pl.BlockSpec((None, n_past, heads, hd), lambda b: (b, 0, 0, 0))],
        out_specs=pl.BlockSpec((None, MOBA_TOP_K, heads, 1), lambda b: (b, 0, 0, 0)),
        out_shape=jax.ShapeDtypeStruct((batch, MOBA_TOP_K, heads, 1), jnp.int32),
        compiler_params=_params("parallel"),
        name="moba_decode_select",
    )(q, kmean)


def _decode_attend_kernel(pt_ref, top_ref, q_ref, kn_ref, vn_ref, ck_ref, cv_ref, o_ref,
                          kbuf_ref, vbuf_ref, sem_ref, *, heads, pages_per_block, n_pages_seq,
                          scale):
    b = pl.program_id(0)
    n_slots = MOBA_TOP_K * pages_per_block

    def head_copies(h):
        copies = []
        for r in range(MOBA_TOP_K):
            block = top_ref[(b * MOBA_TOP_K + r) * heads + h]
            for pg in range(pages_per_block):
                page = pt_ref[b * n_pages_seq + block * pages_per_block + pg]
                slot = r * pages_per_block + pg
                copies.append(pltpu.make_async_copy(
                    ck_ref.at[page, :, h, :], kbuf_ref.at[h, slot], sem_ref.at[0, h]))
                copies.append(pltpu.make_async_copy(
                    cv_ref.at[page, :, h, :], vbuf_ref.at[h, slot], sem_ref.at[1, h]))
        return copies

    all_copies = [head_copies(h) for h in range(heads)]
    for copies in all_copies:
        for cp in copies:
            cp.start()

    for h in range(heads):
        for cp in all_copies[h]:
            cp.wait()
        q = q_ref[h:h + 1, :]
        k_new, v_new = kn_ref[h:h + 1, :], vn_ref[h:h + 1, :]
        s_own = jnp.sum(q * k_new, axis=-1, keepdims=True) * scale
        s_past = [jnp.sum(kbuf_ref[h, i] * q, axis=-1, keepdims=True) * scale
                  for i in range(n_slots)]
        m = s_own
        for s in s_past:
            m = jnp.maximum(m, jnp.max(s, axis=0, keepdims=True))
        p_own = jnp.exp(s_own - m)
        denom = p_own
        acc = p_own * v_new
        for i, s in enumerate(s_past):
            p = jnp.exp(s - m)
            denom = denom + jnp.sum(p, axis=0, keepdims=True)
            acc = acc + jnp.sum(p * vbuf_ref[h, i], axis=0, keepdims=True)
        o_ref[h:h + 1, :] = acc / denom


def decode_attend(q, k_new, v_new, cache_k, cache_v, page_table_flat, top_flat, *, n_pages_seq):
    batch, heads, hd = q.shape
    page = cache_k.shape[1]
    ppb = MOBA_BLOCK // page
    n_slots = MOBA_TOP_K * ppb
    tok_spec = pl.BlockSpec((None, heads, hd), lambda b, pt, tp: (b, 0, 0))
    hbm_spec = pl.BlockSpec(memory_space=pl.ANY)
    return pl.pallas_call(
        functools.partial(_decode_attend_kernel, heads=heads, pages_per_block=ppb,
                          n_pages_seq=n_pages_seq, scale=hd ** -0.5),
        grid_spec=pltpu.PrefetchScalarGridSpec(
            num_scalar_prefetch=2,
            grid=(batch,),
            in_specs=[tok_spec, tok_spec, tok_spec, hbm_spec, hbm_spec],
            out_specs=tok_spec,
            scratch_shapes=[pltpu.VMEM((heads, n_slots, page, hd), F32),
                            pltpu.VMEM((heads, n_slots, page, hd), F32),
                            pltpu.SemaphoreType.DMA((2, heads))],
        ),
        out_shape=jax.ShapeDtypeStruct((batch, heads, hd), F32),
        compiler_params=_params("arbitrary"),
        name="moba_decode_attend",
    )(page_table_flat, top_flat, q, k_new, v_new, cache_k, cache_v)


def _weights_needed_after(layer, depth, n_gla):
    if layer + 1 >= depth:
        return []
    nxt = layer + 1
    needed = [("up", nxt), ("down", nxt)]
    if nxt < n_gla:
        needed.append(("out", nxt))
    else:
        needed += [("q", nxt - n_gla), ("o", nxt - n_gla)]
    if layer == n_gla - 1:
        needed.append(("kv", 0))
    return needed


def _trunk(x, w, bf16_w, *, batch, seq, raw_w=None, gla_s0=None, cache=None, kmean=None,
           kmean_side=None):
    n_gla, depth = w["n_gla"], w["depth"]
    heads = MOBA_HEADS
    h = x
    width = x.shape[1]
    hd = width // heads
    states = []
    kmean_parts = []
    k_new = v_new = None
    for l in range(depth):
        if l < n_gla:
            proj, = norm_matmul(h, w["norm_mix"][l], w["w_qkvr"], l, tn=1536)
            b = gla_gate(h, w["norm_mix"][l], w["w_z"], w["w_gate"], w["b_gate"], l,
                         chunk=min(GLA_CHUNK, seq))
            if cache is None:
                todo = [("up", l), ("down", l)] if raw_w is not None and l == 0 else []
                og, s, *cast = gla_prompt(proj, b, w["head_norm"], l, batch=batch, seq=seq,
                                          casts=[(raw_w[name], index) for name, index in todo])
                bf16_w.update(zip(todo, cast))
            else:
                og, s = gla_step(proj, b, w["head_norm"], gla_s0, l)
            states.append(s)
            h = matmul_residual(og, bf16_w["out", l], h, 0)
        else:
            i = l - n_gla
            q, = norm_matmul(h, w["norm_mix"][l], bf16_w["q", i], 0, tm=512, tn=2048)
            if cache is None:
                o = moba_prompt(q, k_new, v_new, batch=batch, seq=seq)
            else:
                cache_k, cache_v, page_table_flat, n_pages = cache
                q3 = q.reshape(batch, heads, hd)
                top = decode_select(q3, kmean)
                o = decode_attend(q3, k_new.reshape(batch, heads, hd),
                                  v_new.reshape(batch, heads, hd), cache_k, cache_v,
                                  page_table_flat, top.reshape(-1),
                                  n_pages_seq=n_pages).reshape(batch, width)
            h = matmul_residual(o, bf16_w["o", i], h, 0)
        todo = _weights_needed_after(l, depth, n_gla) if raw_w is not None else []
        side = None
        if kmean_side is not None:
            side = (*kmean_side, sum(part.shape[0] for part in kmean_parts))
        results = list(mlp_residual(
            h, w["norm_mlp"][l:l + 1], bf16_w["up", l], bf16_w["down", l], 0,
            final_gain=w["norm_final"] if l == depth - 1 else None,
            casts=[(raw_w[name], index) for name, index in todo], kmean_side=side))
        h = results.pop(0)
        if side is not None:
            kmean_parts.append(results.pop(0))
        for key, cast in zip(todo, results):
            bf16_w[key] = cast
        if l == n_gla - 1:
            k_new, v_new = norm_matmul(h, w["kv_norm"], bf16_w["kv", 0], 0, n_split=2, tn=512)
    return h, jnp.stack(states), k_new, v_new, kmean_parts


def kernel(x_prompt, x_sample, state_gla, cache_k, cache_v, page_table, norm_mix, norm_mlp,
           w_mlp_up, w_mlp_down, w_in_a, w_gate_a, b_gate_a, head_norm_a, w_out_a, kv_norm,
           w_kv, w_q_b, w_o_b, norm_final):
    batch, seq, d = x_prompt.shape
    dec_batch, dec_seq, _ = x_sample.shape
    assert dec_seq == 1
    n_qkvr = w_in_a.shape[2] - GLA_GATE_RANK
    w = dict(
        depth=w_mlp_up.shape[0], n_gla=w_in_a.shape[0], n_qkvr=n_qkvr,
        norm_mix=norm_mix, norm_mlp=norm_mlp, norm_final=norm_final, kv_norm=kv_norm,
        head_norm=head_norm_a, w_gate=w_gate_a, b_gate=b_gate_a, w_z=w_in_a[:, :, n_qkvr:],
    )
    raw_w = {"out": w_out_a, "up": w_mlp_up, "down": w_mlp_down, "kv": w_kv[None],
             "q": w_q_b, "o": w_o_b}
    w["w_qkvr"] = transpose_cast(jnp.swapaxes(w_in_a, 1, 2), n_qkvr)
    bf16_w = {("out", 0): w_out_a[0:1].astype(BF16)}
    heads, hd = cache_k.shape[2], cache_k.shape[3]

    page_table_flat = page_table.reshape(-1)
    n_pages = page_table.shape[1]
    n_past = n_pages * cache_k.shape[1] // MOBA_BLOCK
    y_p, s_p, k_p, v_p, kmean_parts = _trunk(
        x_prompt.reshape(batch * seq, d), w, bf16_w, batch=batch, seq=seq, raw_w=raw_w,
        kmean_side=(cache_k, page_table_flat))
    kmean = jnp.concatenate(kmean_parts, axis=0)
    assert kmean.shape[0] == dec_batch * n_past, "prompt MLP grid steps must cover the cached blocks"
    cache = (cache_k, cache_v, page_table_flat, n_pages)
    y_s, s_s, k_s, v_s, _ = _trunk(
        x_sample.reshape(dec_batch, d), w, bf16_w, batch=dec_batch, seq=1, gla_s0=state_gla,
        cache=cache, kmean=kmean.reshape(dec_batch, n_past, heads, hd))
    return (y_p.reshape(batch, seq, d), y_s.reshape(dec_batch, 1, d), s_p, s_s,
            k_p.reshape(batch, seq, heads, hd), v_p.reshape(batch, seq, heads, hd),
            k_s.reshape(dec_batch, 1, heads, hd), v_s.reshape(dec_batch, 1, heads, hd))
```

```python
import functools

import jax
import jax.numpy as jnp
from jax import lax
from jax.experimental import pallas as pl
from jax.experimental.pallas import tpu as pltpu

F32 = jnp.float32
BF16 = jnp.bfloat16
HIGHEST = lax.Precision.HIGHEST

RMS_EPS = 1e-6
GLA_HEADS = 4
GLA_GATE_RANK = 16
GLA_GATE_TAU = 16.0
GLA_CHUNK = 64
GLA_SUB = 16
MOBA_HEADS = 16
MOBA_BLOCK = 256
MOBA_TOP_K = 3
LOG2_E = 1.4426950408889634
MASK_BIAS = 2.0 ** 100

V7X_VMEM_BYTES = 64 * 1024 * 1024
VMEM_LIMIT_BYTES = V7X_VMEM_BYTES - 8 * 1024 * 1024
BF16_SUBLANES = 16


def _params(*semantics):
    return pltpu.CompilerParams(dimension_semantics=semantics,
                                vmem_limit_bytes=VMEM_LIMIT_BYTES)


def _rms_normalize(x, gain):
    ms = jnp.mean(x * x, axis=-1, keepdims=True)
    return x * lax.rsqrt(ms + RMS_EPS) * gain


def _row_tile(m, want):
    return want if m % want == 0 else m


def _col_tile(n, want):
    while n % want:
        want //= 2
    return want


def _contract_last(a, b, **kw):
    return lax.dot_general(a, b, (((1,), (1,)), ((), ())), preferred_element_type=F32, **kw)


def _contract_first(a, b):
    return lax.dot_general(a, b, (((0,), (0,)), ((), ())), preferred_element_type=F32)


def _transpose_cast_kernel(wt_ref, o_ref):
    o_ref[...] = wt_ref[...].T.astype(o_ref.dtype)


def transpose_cast(w_t, n_cols, *, tn=512):
    layers, _, d = w_t.shape
    assert n_cols % tn == 0
    return pl.pallas_call(
        _transpose_cast_kernel,
        grid=(layers, n_cols // tn),
        in_specs=[pl.BlockSpec((None, tn, d), lambda l, j: (l, j, 0))],
        out_specs=pl.BlockSpec((None, d, tn), lambda l, j: (l, 0, j)),
        out_shape=jax.ShapeDtypeStruct((layers, d, n_cols), BF16),
        compiler_params=_params("parallel", "parallel"),
        name="transpose_cast",
    )(w_t)


def _norm_matmul_kernel(x_ref, gain_ref, *refs, n_split):
    w_refs, o_refs, xn_ref = refs[:n_split], refs[n_split:2 * n_split], refs[2 * n_split]

    @pl.when(pl.program_id(1) == 0)
    def _():
        xn_ref[...] = _rms_normalize(x_ref[...], gain_ref[...]).astype(xn_ref.dtype)

    for w_ref, o_ref in zip(w_refs, o_refs):
        o_ref[...] = jnp.dot(xn_ref[...], w_ref[...], preferred_element_type=F32)


def norm_matmul(x, gain, w, layer, *, n_split=1, tm=1024, tn=1024):
    m, d = x.shape
    n = w.shape[2]
    n_out = n // n_split
    tm = _row_tile(m, tm)
    tn = _col_tile(n_out, tn)
    per_split = n_out // tn

    def w_spec(s):
        return pl.BlockSpec((None, d, tn), lambda i, j: (layer, 0, s * per_split + j))

    return pl.pallas_call(
        functools.partial(_norm_matmul_kernel, n_split=n_split),
        grid=(m // tm, per_split),
        in_specs=[pl.BlockSpec((tm, d), lambda i, j: (i, 0)),
                  pl.BlockSpec((1, d), lambda i, j: (0, 0))]
                 + [w_spec(s) for s in range(n_split)],
        out_specs=[pl.BlockSpec((tm, tn), lambda i, j: (i, j))] * n_split,
        out_shape=[jax.ShapeDtypeStruct((m, n_out), F32)] * n_split,
        scratch_shapes=[pltpu.VMEM((tm, d), BF16)],
        compiler_params=_params("parallel", "arbitrary"),
        name="norm_matmul",
    )(x, gain.reshape(1, d), *([w] * n_split))


def _split_bf16(a):
    hi = a.astype(BF16)
    return hi, a - hi.astype(F32)


def _gla_gate_kernel(x_ref, gain_ref, wz_ref, wg_ref, bg_ref, o_ref, *, chunk):
    xn = _rms_normalize(x_ref[...], gain_ref[...])
    z = jnp.dot(xn.astype(BF16), wz_ref[...].astype(BF16), preferred_element_type=F32)
    logit = jnp.dot(z.astype(BF16), wg_ref[...].astype(BF16),
                    preferred_element_type=F32) + bg_ref[...]
    log_sig = jnp.minimum(logit, 0.0) - jnp.log(1.0 + jnp.exp(-jnp.abs(logit)))
    g = log_sig / GLA_GATE_TAU
    if chunk > 1:
        rows = g.shape[0]
        row = lax.broadcasted_iota(jnp.int32, (rows, rows), 0)
        col = lax.broadcasted_iota(jnp.int32, (rows, rows), 1)
        same_chunk_past = (row >= col) & (row // chunk == col // chunk)
        ones = jnp.where(same_chunk_past, 1.0, 0.0).astype(BF16)
        g_hi, g_rest = _split_bf16(g)
        g_mid, g_rest = _split_bf16(g_rest)
        g = (jnp.dot(ones, g_hi, preferred_element_type=F32)
             + jnp.dot(ones, g_mid, preferred_element_type=F32)
             + jnp.dot(ones, g_rest.astype(BF16), preferred_element_type=F32))
    o_ref[...] = g


def gla_gate(x, gain, w_z, w_gate, b_gate, layer, *, chunk, tm=256):
    m, d = x.shape
    rank, n = w_gate.shape[1:]
    tm = _row_tile(m, tm)
    assert chunk == 1 or tm % chunk == 0
    return pl.pallas_call(
        functools.partial(_gla_gate_kernel, chunk=chunk),
        grid=(m // tm,),
        in_specs=[
            pl.BlockSpec((tm, d), lambda i: (i, 0)),
            pl.BlockSpec((1, d), lambda i: (0, 0)),
            pl.BlockSpec((None, d, rank), lambda i: (layer, 0, 0)),
            pl.BlockSpec((None, rank, n), lambda i: (layer, 0, 0)),
            pl.BlockSpec((None, 1, n), lambda i: (layer, 0, 0)),
        ],
        out_specs=pl.BlockSpec((tm, n), lambda i: (i, 0)),
        out_shape=jax.ShapeDtypeStruct((m, n), F32),
        compiler_params=_params("parallel"),
        name="gla_gate",
    )(x, gain.reshape(1, d), w_z, w_gate, b_gate.reshape(b_gate.shape[0], 1, n))


def _gla_chunk(q, k, v, b, st):
    c, dk = q.shape
    n_sub = c // GLA_SUB
    row = lax.broadcasted_iota(jnp.int32, (c, c), 0)
    col = lax.broadcasted_iota(jnp.int32, (c, c), 1)

    def rows_of(r, n):
        return jnp.concatenate(
            [jnp.broadcast_to(b[j * GLA_SUB + r:j * GLA_SUB + r + 1, :], (GLA_SUB, dk))
             for j in range(n)], axis=0)

    b_end = b[c - 1:c, :]
    v_b = v.astype(BF16)

    k_end = (k * jnp.exp(b_end - b)).astype(BF16)
    st_new = st * jnp.exp(b_end) + _contract_first(v_b, k_end)

    from_start = b - rows_of(0, n_sub)
    q_diag = q * jnp.exp(from_start)
    k_diag = k * jnp.exp(-from_start)
    att_diag = _contract_last(q_diag.astype(BF16), k_diag.astype(BF16))
    same_sub = (row // GLA_SUB) == (col // GLA_SUB)
    att = jnp.where(same_sub & (row >= col), att_diag, 0.0)

    if n_sub > 1:
        past = c - GLA_SUB
        k_off = (k[:past] * jnp.exp(rows_of(GLA_SUB - 1, n_sub - 1) - b[:past])).astype(BF16)
        q_parts, k_parts = [], []
        for j in range(n_sub - 1):
            lo = (j + 1) * GLA_SUB
            q_j = (q[lo:] * jnp.exp(b[lo:] - b[lo - 1:lo, :])).astype(BF16)
            q_parts.append(jnp.concatenate([jnp.zeros((lo, dk), BF16), q_j], axis=0))
            pieces = [k_off[lo - GLA_SUB:lo], jnp.zeros((c - lo, dk), BF16)]
            if j:
                pieces.insert(0, jnp.zeros((lo - GLA_SUB, dk), BF16))
            k_parts.append(jnp.concatenate(pieces, axis=0))
        att = att + _contract_last(jnp.concatenate(q_parts, axis=1),
                                   jnp.concatenate(k_parts, axis=1))

    o = _contract_last((q * jnp.exp(b)).astype(BF16), st.astype(BF16))
    o = o + jnp.dot(att.astype(BF16), v_b, preferred_element_type=F32)
    return o, st_new


def _gla_prompt_kernel(q_ref, k_ref, v_ref, r_ref, b_ref, hn_ref, *refs, n_casts, n_chunks,
                       q_scale, unroll):
    cast_src, (og_ref, s_ref) = refs[:n_casts], refs[n_casts:n_casts + 2]
    cast_dst, st_ref = refs[n_casts + 2:2 * n_casts + 2], refs[2 * n_casts + 2]
    t = pl.program_id(2)
    for src, dst in zip(cast_src, cast_dst):
        dst[...] = src[...].astype(dst.dtype)

    @pl.when(t == 0)
    def _():
        st_ref[...] = jnp.zeros_like(st_ref)

    def chunk(ci, carry):
        rows = pl.ds(pl.multiple_of(ci * GLA_CHUNK, GLA_CHUNK), GLA_CHUNK)
        o, st_new = _gla_chunk(q_ref[rows, :] * q_scale, k_ref[rows, :], v_ref[rows, :],
                               b_ref[rows, :], st_ref[...])
        st_ref[...] = st_new
        r = r_ref[rows, :]
        silu = r / (1.0 + jnp.exp(-r))
        og_ref[rows, :] = (_rms_normalize(o, hn_ref[...]) * silu).astype(og_ref.dtype)
        return carry

    lax.fori_loop(0, n_chunks, chunk, 0, unroll=unroll)

    @pl.when(t == pl.num_programs(2) - 1)
    def _():
        s_ref[...] = st_ref[...].T


def _cast_side_stream(casts, grid):
    n_steps = 1
    for extent in grid:
        n_steps *= extent

    def step_of(*ids):
        step = 0
        for idx, extent in zip(ids, grid):
            step = step * extent + idx
        return step

    operands, in_specs, out_specs, out_shapes = [], [], [], []
    for src, src_layer in casts:
        _, rows, cols = src.shape
        slab = rows // n_steps
        assert rows == slab * n_steps and slab % BF16_SUBLANES == 0
        operands.append(src)
        in_specs.append(pl.BlockSpec(
            (None, slab, cols),
            lambda *ids, src_layer=src_layer: (src_layer, step_of(*ids[:len(grid)]), 0)))
        out_specs.append(pl.BlockSpec((None, slab, cols),
                                      lambda *ids: (0, step_of(*ids[:len(grid)]), 0)))
        out_shapes.append(jax.ShapeDtypeStruct((1, rows, cols), BF16))
    return operands, in_specs, out_specs, out_shapes


def gla_prompt(proj, b, head_norm, layer, *, batch, seq, tt=1024, unroll=8, casts=()):
    m = batch * seq
    qk = b.shape[1]
    dk = qk // GLA_HEADS
    vw = (proj.shape[1] - 2 * qk) // 2
    dv = vw // GLA_HEADS
    tt = min(tt, seq)
    assert seq % tt == 0 and tt % GLA_CHUNK == 0 and dv % dk == 0
    nt = seq // tt
    k_col0 = qk // dk
    v_col0 = 2 * qk // dv
    r_col0 = (2 * qk + vw) // dv
    grid = (batch, GLA_HEADS, nt)
    cast_operands, cast_in, cast_out, cast_shapes = _cast_side_stream(casts, grid)
    kern = functools.partial(_gla_prompt_kernel, n_casts=len(casts),
                             n_chunks=tt // GLA_CHUNK, q_scale=dk ** -0.5, unroll=unroll)
    return pl.pallas_call(
        kern,
        grid=grid,
        in_specs=[
            pl.BlockSpec((tt, dk), lambda bi, h, t: (bi * nt + t, h)),
            pl.BlockSpec((tt, dk), lambda bi, h, t: (bi * nt + t, k_col0 + h)),
            pl.BlockSpec((tt, dv), lambda bi, h, t: (bi * nt + t, v_col0 + h)),
            pl.BlockSpec((tt, dv), lambda bi, h, t: (bi * nt + t, r_col0 + h)),
            pl.BlockSpec((tt, dk), lambda bi, h, t: (bi * nt + t, h)),
            pl.BlockSpec((None, 1, dv), lambda bi, h, t: (layer, 0, 0)),
        ] + cast_in,
        out_specs=[
            pl.BlockSpec((tt, dv), lambda bi, h, t: (bi * nt + t, h)),
            pl.BlockSpec((None, None, dk, dv), lambda bi, h, t: (bi, h, 0, 0)),
        ] + cast_out,
        out_shape=[
            jax.ShapeDtypeStruct((m, vw), BF16),
            jax.ShapeDtypeStruct((batch, GLA_HEADS, dk, dv), F32),
        ] + cast_shapes,
        scratch_shapes=[pltpu.VMEM((dv, dk), F32)],
        compiler_params=_params("parallel", "parallel", "arbitrary"),
        name="gla_prompt",
    )(proj, proj, proj, proj, b, head_norm.reshape(head_norm.shape[0], 1, dv), *cast_operands)


def _gla_step_kernel(qkg_ref, v_ref, r_ref, hn_ref, s0_ref, og_ref, s_ref, *, q_scale):
    for h in range(GLA_HEADS):
        q, k, g = (qkg_ref[h, :, i:i + 1] for i in range(3))
        s_new = jnp.exp(g) * s0_ref[h] + k * v_ref[h]
        s_ref[h] = s_new
        o = jnp.sum((q * q_scale) * s_new, axis=0, keepdims=True)
        r = r_ref[h]
        silu = r / (1.0 + jnp.exp(-r))
        og_ref[h] = _rms_normalize(o, hn_ref[...]) * silu


def gla_step(proj, g, head_norm, s0, layer):
    batch = proj.shape[0]
    qk = g.shape[1]
    dk = qk // GLA_HEADS
    vw = (proj.shape[1] - 2 * qk) // 2
    dv = vw // GLA_HEADS
    qkg = jnp.stack([proj[:, :qk], proj[:, qk:2 * qk], g], axis=-1)
    qkg = qkg.reshape(batch, GLA_HEADS, dk, 3)
    row = lambda a: a.reshape(batch, GLA_HEADS, 1, dv)
    v_r = row(proj[:, 2 * qk:2 * qk + vw])
    r_r = row(proj[:, 2 * qk + vw:])
    row_spec = pl.BlockSpec((None, GLA_HEADS, 1, dv), lambda bi: (bi, 0, 0, 0))
    og, s = pl.pallas_call(
        functools.partial(_gla_step_kernel, q_scale=dk ** -0.5),
        grid=(batch,),
        in_specs=[pl.BlockSpec((None, GLA_HEADS, dk, 3), lambda bi: (bi, 0, 0, 0)),
                  row_spec, row_spec,
                  pl.BlockSpec((None, 1, dv), lambda bi: (layer, 0, 0)),
                  pl.BlockSpec((None, None, GLA_HEADS, dk, dv), lambda bi: (layer, bi, 0, 0, 0))],
        out_specs=[row_spec,
                   pl.BlockSpec((None, GLA_HEADS, dk, dv), lambda bi: (bi, 0, 0, 0))],
        out_shape=[jax.ShapeDtypeStruct((batch, GLA_HEADS, 1, dv), F32),
                   jax.ShapeDtypeStruct(s0.shape[1:], F32)],
        compiler_params=_params("parallel"),
        name="gla_step",
    )(qkg, v_r, r_r, head_norm.reshape(head_norm.shape[0], 1, dv), s0)
    return og.reshape(batch, vw), s


def _matmul_residual_kernel(a_ref, w_ref, h_ref, o_ref):
    o_ref[...] = h_ref[...] + jnp.dot(a_ref[...].astype(BF16), w_ref[...],
                                      preferred_element_type=F32)


def matmul_residual(a, w, h, layer, *, tm=512, tn=2048):
    m, kd = a.shape
    n = w.shape[2]
    tm = _row_tile(m, tm)
    tn = _col_tile(n, tn)
    return pl.pallas_call(
        _matmul_residual_kernel,
        grid=(m // tm, n // tn),
        in_specs=[
            pl.BlockSpec((tm, kd), lambda i, j: (i, 0)),
            pl.BlockSpec((None, kd, tn), lambda i, j: (layer, 0, j)),
            pl.BlockSpec((tm, tn), lambda i, j: (i, j)),
        ],
        out_specs=pl.BlockSpec((tm, tn), lambda i, j: (i, j)),
        out_shape=jax.ShapeDtypeStruct((m, n), F32),
        compiler_params=_params("parallel", "parallel"),
        name="matmul_residual",
    )(a, w, h)


def _mlp_kernel(*refs, has_final_gain, n_casts, kmean):
    refs = list(refs)
    pt_ref = refs.pop(0) if kmean else None
    h_ref, gain_ref, wu_ref, wd_ref = refs[:4]
    del refs[:4]
    final_gain_ref = refs.pop(0) if has_final_gain else None
    cast_src = [refs.pop(0) for _ in range(n_casts)]
    ck_ref = refs.pop(0) if kmean else None
    o_ref = refs.pop(0)
    km_ref = refs.pop(0) if kmean else None
    cast_dst = [refs.pop(0) for _ in range(n_casts)]
    xn_ref = refs.pop(0)

    f, n_f = pl.program_id(1), pl.num_programs(1)
    if kmean:
        first_block, pages_per_block, block_len = kmean
        pages_ref, sem_ref = refs
        step = pl.program_id(0) * n_f + f
        slot = step % 2

        def page_copies(s, sl):
            first_page = (first_block + s) * pages_per_block
            return [pltpu.make_async_copy(ck_ref.at[pt_ref[first_page + pg]],
                                          pages_ref.at[sl, pg], sem_ref.at[sl])
                    for pg in range(pages_per_block)]

        @pl.when(step == 0)
        def _():
            for cp in page_copies(0, 0):
                cp.start()

        @pl.when(step + 1 < pl.num_programs(0) * n_f)
        def _():
            for cp in page_copies(step + 1, 1 - slot):
                cp.start()

        for cp in page_copies(step, slot):
            cp.wait()

    @pl.when(f == 0)
    def _():
        h = h_ref[...]
        xn_ref[...] = _rms_normalize(h, gain_ref[...]).astype(xn_ref.dtype)
        o_ref[...] = h

    u = jnp.maximum(jnp.dot(xn_ref[...], wu_ref[...], preferred_element_type=F32), 0.0)
    o_ref[...] += jnp.dot((u * u).astype(BF16), wd_ref[...], preferred_element_type=F32)

    if has_final_gain:
        @pl.when(f == n_f - 1)
        def _():
            o_ref[...] = _rms_normalize(o_ref[...], final_gain_ref[...])

    for src, dst in zip(cast_src, cast_dst):
        dst[...] = src[...].astype(dst.dtype)

    if kmean:
        total = jnp.sum(pages_ref[slot, 0], axis=0)
        for pg in range(1, pages_per_block):
            total = total + jnp.sum(pages_ref[slot, pg], axis=0)
        km_ref[...] = total / block_len


def mlp_residual(h, gain, w_up, w_down, layer, *, tm=512, tf=1024, final_gain=None,
                 casts=(), kmean_side=None):
    m, d = h.shape
    ff = w_up.shape[2]
    tm = _row_tile(m, tm)
    tf = _col_tile(ff, tf)
    grid = (m // tm, ff // tf)
    n_steps = grid[0] * grid[1]
    step_of = lambda i, f: i * grid[1] + f

    operands = [h, gain.reshape(gain.shape[0], 1, d), w_up, w_down]
    in_specs = [
        pl.BlockSpec((tm, d), lambda i, f, *_: (i, 0)),
        pl.BlockSpec((None, 1, d), lambda i, f, *_: (layer, 0, 0)),
        pl.BlockSpec((None, d, tf), lambda i, f, *_: (layer, 0, f)),
        pl.BlockSpec((None, tf, d), lambda i, f, *_: (layer, f, 0)),
    ]
    out_specs = [pl.BlockSpec((tm, d), lambda i, f, *_: (i, 0))]
    out_shapes = [jax.ShapeDtypeStruct((m, d), F32)]
    scratch = [pltpu.VMEM((tm, d), BF16)]
    if final_gain is not None:
        operands.append(final_gain.reshape(1, d))
        in_specs.append(pl.BlockSpec((1, d), lambda i, f, *_: (0, 0)))
    cast_operands, cast_in, cast_out_specs, cast_out_shapes = _cast_side_stream(casts, grid)
    operands += cast_operands
    in_specs += cast_in
    kmean = None
    if kmean_side is not None:
        cache_k, page_table_flat, first_block = kmean_side
        _, page, heads, hd = cache_k.shape
        ppb = MOBA_BLOCK // page
        kmean = (first_block, ppb, MOBA_BLOCK)
        operands = [page_table_flat] + operands + [cache_k]
        in_specs.append(pl.BlockSpec(memory_space=pl.ANY))
        out_specs.append(pl.BlockSpec((None, heads, hd), lambda i, f, *_: (step_of(i, f), 0, 0)))
        out_shapes.append(jax.ShapeDtypeStruct((n_steps, heads, hd), F32))
        scratch += [pltpu.VMEM((2, ppb, page, heads, hd), F32), pltpu.SemaphoreType.DMA((2,))]
    kern = functools.partial(_mlp_kernel, has_final_gain=final_gain is not None,
                             n_casts=len(casts), kmean=kmean)
    return pl.pallas_call(
        kern,
        grid_spec=pltpu.PrefetchScalarGridSpec(
            num_scalar_prefetch=1 if kmean else 0, grid=grid, in_specs=in_specs,
            out_specs=out_specs + cast_out_specs, scratch_shapes=scratch),
        out_shape=out_shapes + cast_out_shapes,
        compiler_params=_params("arbitrary" if kmean else "parallel", "arbitrary"),
        name="mlp_residual",
    )(*operands)


def _top_blocks(gate, is_past, axis):
    n_blocks = gate.shape[axis]
    idx = lax.broadcasted_iota(jnp.int32, gate.shape, axis)
    neg_inf = jnp.float32(-jnp.inf)
    left = jnp.where(is_past, gate, neg_inf)
    chosen = jnp.zeros(gate.shape, dtype=jnp.bool_)
    picks = []
    for _ in range(MOBA_TOP_K):
        best = jnp.max(left, axis=axis, keepdims=True)
        is_best = (left == best) & (best > neg_inf)
        first = jnp.min(jnp.where(is_best, idx, n_blocks), axis=axis, keepdims=True)
        pick = idx == first
        chosen = chosen | pick
        left = jnp.where(pick, neg_inf, left)
        picks.append(first)
    return chosen, picks


def _moba_prompt_kernel(q_ref, k_ref, v_ref, *refs, n_casts, n_blocks, n_heads, hd, scale, tq):
    cast_src, o_ref = refs[:n_casts], refs[n_casts]
    cast_dst = refs[n_casts + 1:2 * n_casts + 1]
    kb_ref, vt_ref, kmean_ref = refs[2 * n_casts + 1:]
    for src, dst in zip(cast_src, cast_dst):
        dst[...] = src[...].astype(dst.dtype)
    ti = pl.program_id(2)
    blk = MOBA_BLOCK
    heads = range(n_heads)
    own_blocks = tq // blk

    @pl.when(ti == 0)
    def _():
        feature = lax.broadcasted_iota(jnp.int32, (blk, hd), 1)
        for j in heads:
            cols = slice(j * hd, (j + 1) * hd)
            for n in range(n_blocks):
                k_blk = k_ref[n * blk:(n + 1) * blk, cols]
                one_hot = jnp.where(feature == n, 1.0, 0.0).astype(BF16)
                kb_ref[j, n] = jnp.concatenate([k_blk.astype(BF16), one_hot], axis=1)
                vt_ref[j, n] = v_ref[n * blk:(n + 1) * blk, cols].T.astype(BF16)
                kmean_ref[j, n:n + 1, :] = jnp.mean(k_blk, axis=0, keepdims=True)

    q_pos = ti * tq + lax.broadcasted_iota(jnp.int32, (1, tq), 1)
    q_block = q_pos // blk
    block_id = lax.broadcasted_iota(jnp.int32, (n_blocks, tq), 0)
    key_off = lax.broadcasted_iota(jnp.int32, (blk, tq), 0)
    neg_inf = jnp.float32(-jnp.inf)

    q_bs, carry0 = [], []
    for j in heads:
        q = q_ref[:, j * hd:(j + 1) * hd]
        gate = _contract_last(kmean_ref[j], q, precision=HIGHEST)
        chosen, _ = _top_blocks(gate, block_id < q_block, axis=0)
        bias = jnp.where(chosen | (block_id == q_block), 0.0, -MASK_BIAS)
        bias = jnp.concatenate([bias, jnp.zeros((hd - n_blocks, tq), F32)], axis=0)
        q_bs.append(jnp.concatenate([(q * (scale * LOG2_E)).astype(BF16),
                                     bias.T.astype(BF16)], axis=1))
        carry0.append((jnp.full((1, tq), neg_inf), jnp.zeros((1, tq), F32),
                       jnp.zeros((hd, tq), F32)))

    def key_block(n, carry, causal):
        raw = [_contract_last(kb_ref[j, n], q_bs[j]) for j in heads]
        partial = []
        for j in heads:
            m_run, l_run, _ = carry[j]
            s = raw[j]
            if causal:
                s = jnp.where(key_off <= q_pos - n * blk, s, neg_inf)
            m_new = jnp.maximum(m_run, jnp.max(s, axis=0, keepdims=True))
            alpha = jnp.exp2(m_run - m_new)
            p = jnp.exp2(s - m_new)
            l_new = alpha * l_run + jnp.sum(p, axis=0, keepdims=True)
            partial.append((m_new, l_new, alpha, p.astype(BF16)))
        out = []
        for j in heads:
            m_new, l_new, alpha, p_b = partial[j]
            acc = alpha * carry[j][2] + jnp.dot(vt_ref[j, n], p_b,
                                                preferred_element_type=F32)
            out.append((m_new, l_new, acc))
        return tuple(out)

    first_own = ti * own_blocks
    carry = lax.fori_loop(0, first_own, functools.partial(key_block, causal=False),
                          tuple(carry0))
    for i in range(own_blocks):
        carry = key_block(first_own + i, carry, causal=True)
    for j in heads:
        _, l_fin, acc = carry[j]
        o_ref[:, j * hd:(j + 1) * hd] = (acc / l_fin).T.astype(o_ref.dtype)


def moba_prompt(q, k, v, *, batch, seq, heads_per_step=4, tq=512, casts=()):
    m, width = q.shape
    hd = width // MOBA_HEADS
    tq = min(tq, seq)
    assert seq % tq == 0 and tq % MOBA_BLOCK == 0 and MOBA_HEADS % heads_per_step == 0
    n_blocks = seq // MOBA_BLOCK
    n_tiles = seq // tq
    gw = heads_per_step * hd
    grid = (batch, MOBA_HEADS // heads_per_step, n_tiles)
    cast_operands, cast_in, cast_out, cast_shapes = _cast_side_stream(casts, grid)
    kern = functools.partial(_moba_prompt_kernel, n_casts=len(casts), n_blocks=n_blocks,
                             n_heads=heads_per_step, hd=hd, scale=hd ** -0.5, tq=tq)
    return pl.pallas_call(
        kern,
        grid=grid,
        in_specs=[
            pl.BlockSpec((tq, gw), lambda b, h, i: (b * n_tiles + i, h)),
            pl.BlockSpec((seq, gw), lambda b, h, i: (b, h)),
            pl.BlockSpec((seq, gw), lambda b, h, i: (b, h)),
        ] + cast_in,
        out_specs=[pl.BlockSpec((tq, gw), lambda b, h, i: (b * n_tiles + i, h))] + cast_out,
        out_shape=[jax.ShapeDtypeStruct((m, width), BF16)] + cast_shapes,
        scratch_shapes=[pltpu.VMEM((heads_per_step, n_blocks, MOBA_BLOCK, 2 * hd), BF16),
                        pltpu.VMEM((heads_per_step, n_blocks, hd, MOBA_BLOCK), BF16),
                        pltpu.VMEM((heads_per_step, n_blocks, hd), F32)],
        compiler_params=_params("parallel", "parallel", "arbitrary"),
        name="moba_prompt",
    )(q, k, v, *cast_operands)


def _decode_select_kernel(q_ref, kmean_ref, o_ref):
    gate = jnp.sum(kmean_ref[...] * q_ref[...], axis=-1, keepdims=True)
    _, picks = _top_blocks(gate, jnp.full(gate.shape, True), axis=0)
    for r, first in enumerate(picks):
        o_ref[r] = first[0]


def decode_select(q, kmean):
    batch, n_past, heads, hd = kmean.shape
    assert n_past >= MOBA_TOP_K
    return pl.pallas_call(
        _decode_select_kernel,
        grid=(batch,),
        in_specs=[pl.BlockSpec((None, heads, hd), lambda b: (b, 0, 0)),
                  pl.BlockSpec((None, n_past, heads, hd), lambda b: (b, 0, 0, 0))],
        out_specs=pl.BlockSpec((None, MOBA_TOP_K, heads, 1), lambda b: (b, 0, 0, 0)),
        out_shape=jax.ShapeDtypeStruct((batch, MOBA_TOP_K, heads, 1), jnp.int32),
        compiler_params=_params("parallel"),
        name="moba_decode_select",
    )(q, kmean)


def _decode_attend_kernel(pt_ref, top_ref, q_ref, kn_ref, vn_ref, ck_ref, cv_ref, o_ref,
                          kbuf_ref, vbuf_ref, sem_ref, *, heads, pages_per_block, n_pages_seq,
                          scale):
    b = pl.program_id(0)
    n_slots = MOBA_TOP_K * pages_per_block

    def head_copies(h):
        copies = []
        for r in range(MOBA_TOP_K):
            block = top_ref[(b * MOBA_TOP_K + r) * heads + h]
            for pg in range(pages_per_block):
                page = pt_ref[b * n_pages_seq + block * pages_per_block + pg]
                slot = r * pages_per_block + pg
                copies.append(pltpu.make_async_copy(
                    ck_ref.at[page, :, h, :], kbuf_ref.at[h, slot], sem_ref.at[0, h]))
                copies.append(pltpu.make_async_copy(
                    cv_ref.at[page, :, h, :], vbuf_ref.at[h, slot], sem_ref.at[1, h]))
        return copies

    all_copies = [head_copies(h) for h in range(heads)]
    for copies in all_copies:
        for cp in copies:
            cp.start()

    for h in range(heads):
        for cp in all_copies[h]:
            cp.wait()
        q = q_ref[h:h + 1, :]
        k_new, v_new = kn_ref[h:h + 1, :], vn_ref[h:h + 1, :]
        s_own = jnp.sum(q * k_new, axis=-1, keepdims=True) * scale
        s_past = [jnp.sum(kbuf_ref[h, i] * q, axis=-1, keepdims=True) * scale
                  for i in range(n_slots)]
        m = s_own
        for s in s_past:
            m = jnp.maximum(m, jnp.max(s, axis=0, keepdims=True))
        p_own = jnp.exp(s_own - m)
        denom = p_own
        acc = p_own * v_new
        for i, s in enumerate(s_past):
            p = jnp.exp(s - m)
            denom = denom + jnp.sum(p, axis=0, keepdims=True)
            acc = acc + jnp.sum(p * vbuf_ref[h, i], axis=0, keepdims=True)
        o_ref[h:h + 1, :] = acc / denom


def decode_attend(q, k_new, v_new, cache_k, cache_v, page_table_flat, top_flat, *, n_pages_seq):
    batch, heads, hd = q.shape
    page = cache_k.shape[1]
    ppb = MOBA_BLOCK // page
    n_slots = MOBA_TOP_K * ppb
    tok_spec = pl.BlockSpec((None, heads, hd), lambda b, pt, tp: (b, 0, 0))
    hbm_spec = pl.BlockSpec(memory_space=pl.ANY)
    return pl.pallas_call(
        functools.partial(_decode_attend_kernel, heads=heads, pages_per_block=ppb,
                          n_pages_seq=n_pages_seq, scale=hd ** -0.5),
        grid_spec=pltpu.PrefetchScalarGridSpec(
            num_scalar_prefetch=2,
            grid=(batch,),
            in_specs=[tok_spec, tok_spec, tok_spec, hbm_spec, hbm_spec],
            out_specs=tok_spec,
            scratch_shapes=[pltpu.VMEM((heads, n_slots, page, hd), F32),
                            pltpu.VMEM((heads, n_slots, page, hd), F32),
                            pltpu.SemaphoreType.DMA((2, heads))],
        ),
        out_shape=jax.ShapeDtypeStruct((batch, heads, hd), F32),
        compiler_params=_params("arbitrary"),
        name="moba_decode_attend",
    )(page_table_flat, top_flat, q, k_new, v_new, cache_k, cache_v)


def _weights_needed_after(layer, depth, n_gla):
    if layer + 1 >= depth:
        return []
    nxt = layer + 1
    if nxt < n_gla:
        needed = [("out", nxt)]
    else:
        needed = [("q", nxt - n_gla), ("o", nxt - n_gla)]
    if layer == n_gla - 1:
        needed.append(("kv", 0))
    return needed


def _trunk(x, w, bf16_w, *, batch, seq, raw_w=None, gla_s0=None, cache=None, kmean=None,
           kmean_side=None):
    n_gla, depth = w["n_gla"], w["depth"]
    heads = MOBA_HEADS
    h = x
    width = x.shape[1]
    hd = width // heads
    states = []
    kmean_parts = []
    k_new = v_new = None
    for l in range(depth):
        mlp_todo = [("up", l), ("down", l)] if raw_w is not None else []
        mlp_casts = [(raw_w[name], index) for name, index in mlp_todo]
        if l < n_gla:
            proj, = norm_matmul(h, w["norm_mix"][l], w["w_qkvr"], l, tn=1536)
            b = gla_gate(h, w["norm_mix"][l], w["w_z"], w["w_gate"], w["b_gate"], l,
                         chunk=min(GLA_CHUNK, seq))
            if cache is None:
                og, s, *cast = gla_prompt(proj, b, w["head_norm"], l, batch=batch, seq=seq,
                                          casts=mlp_casts)
                bf16_w.update(zip(mlp_todo, cast))
            else:
                og, s = gla_step(proj, b, w["head_norm"], gla_s0, l)
            states.append(s)
            h = matmul_residual(og, bf16_w["out", l], h, 0)
        else:
            i = l - n_gla
            q, = norm_matmul(h, w["norm_mix"][l], bf16_w["q", i], 0, tm=512, tn=2048)
            if cache is None:
                o, *cast = moba_prompt(q, k_new, v_new, batch=batch, seq=seq, casts=mlp_casts)
                bf16_w.update(zip(mlp_todo, cast))
            else:
                cache_k, cache_v, page_table_flat, n_pages = cache
                q3 = q.reshape(batch, heads, hd)
                top = decode_select(q3, kmean)
                o = decode_attend(q3, k_new.reshape(batch, heads, hd),
                                  v_new.reshape(batch, heads, hd), cache_k, cache_v,
                                  page_table_flat, top.reshape(-1),
                                  n_pages_seq=n_pages).reshape(batch, width)
            h = matmul_residual(o, bf16_w["o", i], h, 0)
        todo = _weights_needed_after(l, depth, n_gla) if raw_w is not None else []
        side = None
        if kmean_side is not None:
            side = (*kmean_side, sum(part.shape[0] for part in kmean_parts))
        results = list(mlp_residual(
            h, w["norm_mlp"][l:l + 1], bf16_w["up", l], bf16_w["down", l], 0,
            final_gain=w["norm_final"] if l == depth - 1 else None,
            casts=[(raw_w[name], index) for name, index in todo], kmean_side=side))
        h = results.pop(0)
        if side is not None:
            kmean_parts.append(results.pop(0))
        for key, cast in zip(todo, results):
            bf16_w[key] = cast
        if l == n_gla - 1:
            k_new, v_new = norm_matmul(h, w["kv_norm"], bf16_w["kv", 0], 0, n_split=2, tn=512)
    return h, jnp.stack(states), k_new, v_new, kmean_parts


def kernel(x_prompt, x_sample, state_gla, cache_k, cache_v, page_table, norm_mix, norm_mlp,
           w_mlp_up, w_mlp_down, w_in_a, w_gate_a, b_gate_a, head_norm_a, w_out_a, kv_norm,
           w_kv, w_q_b, w_o_b, norm_final):
    batch, seq, d = x_prompt.shape
    dec_batch, dec_seq, _ = x_sample.shape
    assert dec_seq == 1
    n_qkvr = w_in_a.shape[2] - GLA_GATE_RANK
    w = dict(
        depth=w_mlp_up.shape[0], n_gla=w_in_a.shape[0], n_qkvr=n_qkvr,
        norm_mix=norm_mix, norm_mlp=norm_mlp, norm_final=norm_final, kv_norm=kv_norm,
        head_norm=head_norm_a, w_gate=w_gate_a, b_gate=b_gate_a, w_z=w_in_a[:, :, n_qkvr:],
    )
    raw_w = {"out": w_out_a, "up": w_mlp_up, "down": w_mlp_down, "kv": w_kv[None],
             "q": w_q_b, "o": w_o_b}
    w["w_qkvr"] = transpose_cast(jnp.swapaxes(w_in_a, 1, 2), n_qkvr)
    bf16_w = {("out", 0): w_out_a[0:1].astype(BF16)}
    heads, hd = cache_k.shape[2], cache_k.shape[3]

    page_table_flat = page_table.reshape(-1)
    n_pages = page_table.shape[1]
    n_past = n_pages * cache_k.shape[1] // MOBA_BLOCK
    y_p, s_p, k_p, v_p, kmean_parts = _trunk(
        x_prompt.reshape(batch * seq, d), w, bf16_w, batch=batch, seq=seq, raw_w=raw_w,
        kmean_side=(cache_k, page_table_flat))
    kmean = jnp.concatenate(kmean_parts, axis=0)
    assert kmean.shape[0] == dec_batch * n_past, "prompt MLP grid steps must cover the cached blocks"
    cache = (cache_k, cache_v, page_table_flat, n_pages)
    y_s, s_s, k_s, v_s, _ = _trunk(
        x_sample.reshape(dec_batch, d), w, bf16_w, batch=dec_batch, seq=1, gla_s0=state_gla,
        cache=cache, kmean=kmean.reshape(dec_batch, n_past, heads, hd))
    return (y_p.reshape(batch, seq, d), y_s.reshape(dec_batch, 1, d), s_p, s_s,
            k_p.reshape(batch, seq, heads, hd), v_p.reshape(batch, seq, heads, hd),
            k_s.reshape(dec_batch, 1, heads, hd), v_s.reshape(dec_batch, 1, heads, hd))
```
